```python
import jax, jax.numpy as jnp
from jax import lax
import numpy as np

D_MODEL = 1024
BATCH = 32
SEQ = 2048
DEPTH = 4
DEC_BATCH = 32
DEC_SEQ = 64
PAST_LEN = 2048

CHUNK = 64
Q_BLOCK = 128
N_MIXERS = 3
N_MLA = (DEPTH + 2) // 3
N_SB = (DEPTH + 1) // 3
N_SWA = DEPTH // 3
RMS_EPS = 1e-6
ROPE_THETA = 10000.0
NEG = -1e30

MLA_HEADS = 16
MLA_NOPE = 128
MLA_ROPE = 64
MLA_V = 128
MLA_Q_LORA = 384
MLA_KV_LORA = 256
MLA_SCALE = (MLA_NOPE + MLA_ROPE) ** -0.5

SB_HEADS = 16
SB_HEAD_DIM = D_MODEL // SB_HEADS
SB_SCALE = SB_HEAD_DIM ** -0.5

SWA_HEADS = 16
SWA_KV_HEADS = 4
SWA_GROUP = SWA_HEADS // SWA_KV_HEADS
SWA_HEAD_DIM = 64
SWA_WINDOW = 128
SWA_CHUNKS_BACK = SWA_WINDOW // CHUNK
SWA_SCALE = SWA_HEAD_DIM ** -0.5

D_FF = 2816
CONV_W = 3
PLE_DIM = 256

kernel_name = 'hybrid_streaming_mla_sb_swa_convffn_step'


def rms_norm(x, g):
    x32 = x.astype(jnp.float32)
    y = x32 * lax.rsqrt(jnp.mean(x32 * x32, axis=-1, keepdims=True) + RMS_EPS)
    return (y * g.astype(jnp.float32)).astype(x.dtype)


def rope(x, pos):
    half = x.shape[-1] // 2
    inv = ROPE_THETA ** (-jnp.arange(half, dtype=jnp.float32) / half)
    ang = pos.astype(jnp.float32)[:, None] * inv[None, :]
    cos = jnp.cos(ang)[None, :, None, :]
    sin = jnp.sin(ang)[None, :, None, :]
    x32 = x.astype(jnp.float32)
    x1, x2 = x32[..., :half], x32[..., half:]
    return jnp.concatenate([x1 * cos - x2 * sin, x2 * cos + x1 * sin], axis=-1).astype(x.dtype)


def to_blocks(a, nb):
    b = a.reshape(a.shape[0], nb, a.shape[1] // nb, *a.shape[2:])
    return jnp.moveaxis(b, 1, 0)


def from_blocks(a):
    a = jnp.moveaxis(a, 0, 1)
    return a.reshape(a.shape[0], a.shape[1] * a.shape[2], *a.shape[3:])


def band_blocks(a, nb):
    pad = [(0, 0), (SWA_WINDOW, 0)] + [(0, 0)] * (a.ndim - 2)
    blk = to_blocks(jnp.pad(a, pad), nb + 1)
    return jnp.concatenate([blk[:-1], blk[1:]], axis=2)


def mla_project(h, pos, w_dq, g_q, w_uq, w_dkv, g_kv, w_uk):
    b, t, _ = h.shape
    cq = rms_norm(h @ w_dq, g_q)
    q = (cq @ w_uq).reshape(b, t, MLA_HEADS, MLA_NOPE + MLA_ROPE)
    q_rope = rope(q[..., MLA_NOPE:], pos)
    q_lat = jnp.einsum('bthn,lhn->bthl', q[..., :MLA_NOPE], w_uk)
    kv = h @ w_dkv
    ckv = rms_norm(kv[..., :MLA_KV_LORA], g_kv)
    krope = rope(kv[..., None, MLA_KV_LORA:], pos)[:, :, 0, :]
    return q_lat, q_rope, ckv, krope


def mla_core(q_lat, q_rope, q_pos, ckv, krope, k_pos, w_uv):
    s = jnp.einsum('bqhl,bkl->bhqk', q_lat, ckv).astype(jnp.float32)
    s = (s + jnp.einsum('bqhr,bkr->bhqk', q_rope, krope).astype(jnp.float32)) * MLA_SCALE
    mask = (k_pos[None, :] // CHUNK) <= (q_pos[:, None] // CHUNK)
    p = jax.nn.softmax(jnp.where(mask, s, NEG), axis=-1).astype(ckv.dtype)
    o_lat = jnp.einsum('bhqk,bkl->bqhl', p, ckv)
    return jnp.einsum('bqhl,lhv->bqhv', o_lat, w_uv)


def mla_mixer(hp, hs, cache_ckv, cache_krope, w_dq, g_q, w_uq, w_dkv, g_kv, w_uk, w_uv, w_o):
    b, s_len, _ = hp.shape
    db, t_len, _ = hs.shape
    nb = s_len // Q_BLOCK
    pos_p = jnp.arange(s_len, dtype=jnp.int32)
    ql, qr, ckv_p, kr_p = mla_project(hp, pos_p, w_dq, g_q, w_uq, w_dkv, g_kv, w_uk)
    o = lax.map(lambda a: mla_core(a[0], a[1], a[2], ckv_p, kr_p, pos_p, w_uv),
                (to_blocks(ql, nb), to_blocks(qr, nb), pos_p.reshape(nb, Q_BLOCK)))
    mix_p = from_blocks(o).reshape(b, s_len, MLA_HEADS * MLA_V) @ w_o
    pos_s = PAST_LEN + jnp.arange(t_len, dtype=jnp.int32)
    ql_s, qr_s, ckv_s, kr_s = mla_project(hs, pos_s, w_dq, g_q, w_uq, w_dkv, g_kv, w_uk)
    ckv_all = jnp.concatenate([cache_ckv.astype(ckv_s.dtype), ckv_s], axis=1)
    kr_all = jnp.concatenate([cache_krope.astype(kr_s.dtype), kr_s], axis=1)
    k_pos = jnp.arange(ckv_all.shape[1], dtype=jnp.int32)
    o_s = mla_core(ql_s, qr_s, pos_s, ckv_all, kr_all, k_pos, w_uv)
    mix_s = o_s.reshape(db, t_len, MLA_HEADS * MLA_V) @ w_o
    return mix_p, mix_s, ckv_p, kr_p, ckv_s, kr_s


def sb_core(q, q_pos, k, v, k_pos):
    z = jnp.einsum('bqhd,bkhd->bhqk', q, k).astype(jnp.float32) * SB_SCALE
    mask = k_pos[None, :] < q_pos[:, None]
    log_not = jnp.where(mask, jax.nn.log_sigmoid(-z), 0.0)
    tail = lax.cumsum(log_not, axis=3, reverse=True) - log_not
    a = jnp.where(mask, jnp.exp(jax.nn.log_sigmoid(z) + tail), 0.0).astype(v.dtype)
    return jnp.einsum('bhqk,bkhd->bqhd', a, v)


def sb_mixer(hp, hs, cache_k, cache_v, w_qkv, w_o):
    def proj(h):
        b, t, _ = h.shape
        qkv = (h @ w_qkv).reshape(b, t, 3, SB_HEADS, SB_HEAD_DIM)
        return qkv[:, :, 0], qkv[:, :, 1], qkv[:, :, 2]
    b, s_len, _ = hp.shape
    db, t_len, _ = hs.shape
    nb = s_len // Q_BLOCK
    pos_p = jnp.arange(s_len, dtype=jnp.int32)
    qp, kp, vp = proj(hp)
    o = lax.map(lambda a: sb_core(a[0], a[1], kp, vp, pos_p),
                (to_blocks(qp, nb), pos_p.reshape(nb, Q_BLOCK)))
    mix_p = from_blocks(o).reshape(b, s_len, SB_HEADS * SB_HEAD_DIM) @ w_o
    qs, ks, vs = proj(hs)
    pos_s = PAST_LEN + jnp.arange(t_len, dtype=jnp.int32)
    k_all = jnp.concatenate([cache_k.astype(ks.dtype), ks], axis=1)
    v_all = jnp.concatenate([cache_v.astype(vs.dtype), vs], axis=1)
    k_pos = jnp.arange(k_all.shape[1], dtype=jnp.int32)
    o_s = sb_core(qs, pos_s, k_all, v_all, k_pos)
    mix_s = o_s.reshape(db, t_len, SB_HEADS * SB_HEAD_DIM) @ w_o
    return mix_p, mix_s, kp, vp, ks, vs


def swa_core(q, q_pos, k, v, k_pos, sinks):
    s = jnp.einsum('bqkgd,bskd->bkgqs', q, k).astype(jnp.float32) * SWA_SCALE
    dc = q_pos[:, None] // CHUNK - k_pos[None, :] // CHUNK
    mask = (dc >= 0) & (dc <= SWA_CHUNKS_BACK) & (k_pos[None, :] >= 0)
    s = jnp.where(mask, s, NEG)
    sink = sinks.astype(jnp.float32).reshape(SWA_KV_HEADS, SWA_GROUP)[None, :, :, None, None]
    m = jnp.maximum(jnp.max(s, axis=-1, keepdims=True), sink)
    e = jnp.exp(s - m)
    p = (e / (jnp.sum(e, axis=-1, keepdims=True) + jnp.exp(sink - m))).astype(v.dtype)
    return jnp.einsum('bkgqs,bskd->bqkgd', p, v)


def swa_mixer(hp, hs, cache_k, cache_v, w_qkv, b_qkv, sinks, w_o):
    nq = SWA_HEADS * SWA_HEAD_DIM
    nk = SWA_KV_HEADS * SWA_HEAD_DIM
    def proj(h, pos):
        b, t, _ = h.shape
        qkv = h @ w_qkv + b_qkv
        q = rope(qkv[..., :nq].reshape(b, t, SWA_HEADS, SWA_HEAD_DIM), pos)
        k = rope(qkv[..., nq:nq + nk].reshape(b, t, SWA_KV_HEADS, SWA_HEAD_DIM), pos)
        v = qkv[..., nq + nk:].reshape(b, t, SWA_KV_HEADS, SWA_HEAD_DIM)
        return q.reshape(b, t, SWA_KV_HEADS, SWA_GROUP, SWA_HEAD_DIM), k, v
    b, s_len, _ = hp.shape
    db, t_len, _ = hs.shape
    nb = s_len // Q_BLOCK
    pos_p = jnp.arange(s_len, dtype=jnp.int32)
    qp, kp, vp = proj(hp, pos_p)
    pos_band = (jnp.arange(nb, dtype=jnp.int32)[:, None] * Q_BLOCK - SWA_WINDOW
                + jnp.arange(SWA_WINDOW + Q_BLOCK, dtype=jnp.int32)[None, :])
    o = lax.map(lambda a: swa_core(a[0], a[1], a[2], a[3], a[4], sinks),
                (to_blocks(qp, nb), pos_p.reshape(nb, Q_BLOCK), band_blocks(kp, nb),
                 band_blocks(vp, nb), pos_band))
    mix_p = from_blocks(o).reshape(b, s_len, nq) @ w_o
    pos_s = PAST_LEN + jnp.arange(t_len, dtype=jnp.int32)
    qs, ks, vs = proj(hs, pos_s)
    c_len = cache_k.shape[1]
    k_all = jnp.concatenate([cache_k.astype(ks.dtype), ks], axis=1)
    v_all = jnp.concatenate([cache_v.astype(vs.dtype), vs], axis=1)
    k_pos = PAST_LEN - c_len + jnp.arange(c_len + t_len, dtype=jnp.int32)
    o_s = swa_core(qs, pos_s, k_all, v_all, k_pos, sinks)
    mix_s = o_s.reshape(db, t_len, nq) @ w_o
    return (mix_p, mix_s, kp[:, -SWA_WINDOW:], vp[:, -SWA_WINDOW:],
            k_all[:, -SWA_WINDOW:], v_all[:, -SWA_WINDOW:])


def conv_ffn(h, conv_state, w_in, conv_w, conv_b, w_out):
    u = h @ w_in
    gate, up = u[..., :D_FF], u[..., D_FF:]
    gx = jnp.concatenate([conv_state.astype(gate.dtype), gate], axis=1)
    conv = lax.conv_general_dilated(gx, conv_w[:, None, :].astype(gx.dtype), window_strides=(1,),
                                    padding='VALID', dimension_numbers=('NWC', 'WIO', 'NWC'),
                                    feature_group_count=D_FF) + conv_b
    y = (jax.nn.gelu(conv, approximate=False) * up) @ w_out
    return y, gx[:, -(CONV_W - 1):]


def ple(x, p, g, w_gate, w_proj):
    return jax.nn.sigmoid(rms_norm(x, g) @ w_gate) * (p @ w_proj)


def setup_inputs(seed: int = 0) -> dict:
    key = jax.random.key(seed)
    ks = list(jax.random.split(key, 40))
    def nrm(shape, scale=1.0):
        return scale * jax.random.normal(ks.pop(), shape, jnp.float32)
    def gain(shape):
        return 1.0 + 0.01 * nrm(shape)
    swa_cache = min(SWA_WINDOW, PAST_LEN)
    qkv_w = (SWA_HEADS + 2 * SWA_KV_HEADS) * SWA_HEAD_DIM
    return {
        'x_prompt': nrm((BATCH, SEQ, D_MODEL)),
        'x_sample': nrm((DEC_BATCH, DEC_SEQ, D_MODEL)),
        'p_prompt': nrm((DEPTH, BATCH, SEQ, PLE_DIM)),
        'p_sample': nrm((DEPTH, DEC_BATCH, DEC_SEQ, PLE_DIM)),
        'cache_mla_ckv': nrm((N_MLA, DEC_BATCH, PAST_LEN, MLA_KV_LORA)),
        'cache_mla_krope': nrm((N_MLA, DEC_BATCH, PAST_LEN, MLA_ROPE)),
        'cache_sb_k': nrm((N_SB, DEC_BATCH, PAST_LEN, SB_HEADS, SB_HEAD_DIM)),
        'cache_sb_v': nrm((N_SB, DEC_BATCH, PAST_LEN, SB_HEADS, SB_HEAD_DIM)),
        'cache_swa_k': nrm((N_SWA, DEC_BATCH, swa_cache, SWA_KV_HEADS, SWA_HEAD_DIM)),
        'cache_swa_v': nrm((N_SWA, DEC_BATCH, swa_cache, SWA_KV_HEADS, SWA_HEAD_DIM)),
        'state_ffn_conv': nrm((DEPTH, DEC_BATCH, CONV_W - 1, D_FF)),
        'g_mix': gain((DEPTH, D_MODEL)),
        'g_ffn': gain((DEPTH, D_MODEL)),
        'g_ple': gain((DEPTH, D_MODEL)),
        'g_final': gain((D_MODEL,)),
        'w_mla_dq': nrm((N_MLA, D_MODEL, MLA_Q_LORA), D_MODEL ** -0.5),
        'g_mla_q': gain((N_MLA, MLA_Q_LORA)),
        'w_mla_uq': nrm((N_MLA, MLA_Q_LORA, MLA_HEADS * (MLA_NOPE + MLA_ROPE)), MLA_Q_LORA ** -0.5),
        'w_mla_dkv': nrm((N_MLA, D_MODEL, MLA_KV_LORA + MLA_ROPE), D_MODEL ** -0.5),
        'g_mla_kv': gain((N_MLA, MLA_KV_LORA)),
        'w_mla_uk': nrm((N_MLA, MLA_KV_LORA, MLA_HEADS, MLA_NOPE), MLA_KV_LORA ** -0.5),
        'w_mla_uv': nrm((N_MLA, MLA_KV_LORA, MLA_HEADS, MLA_V), MLA_KV_LORA ** -0.5),
        'w_mla_o': nrm((N_MLA, MLA_HEADS * MLA_V, D_MODEL), (MLA_HEADS * MLA_V) ** -0.5),
        'w_sb_qkv': nrm((N_SB, D_MODEL, 3 * SB_HEADS * SB_HEAD_DIM), D_MODEL ** -0.5),
        'w_sb_o': nrm((N_SB, SB_HEADS * SB_HEAD_DIM, D_MODEL), (SB_HEADS * SB_HEAD_DIM) ** -0.5),
        'w_swa_qkv': nrm((N_SWA, D_MODEL, qkv_w), D_MODEL ** -0.5),
        'b_swa_qkv': nrm((N_SWA, qkv_w), 0.02),
        'swa_sinks': nrm((N_SWA, SWA_HEADS), 0.5),
        'w_swa_o': nrm((N_SWA, SWA_HEADS * SWA_HEAD_DIM, D_MODEL), (SWA_HEADS * SWA_HEAD_DIM) ** -0.5),
        'w_ffn_in': nrm((DEPTH, D_MODEL, 2 * D_FF), D_MODEL ** -0.5),
        'ffn_conv_w': nrm((DEPTH, CONV_W, D_FF), CONV_W ** -0.5),
        'ffn_conv_b': nrm((DEPTH, D_FF), 0.02),
        'w_ffn_out': nrm((DEPTH, D_FF, D_MODEL), D_FF ** -0.5),
        'w_ple_gate': nrm((DEPTH, D_MODEL, D_MODEL), D_MODEL ** -0.5),
        'w_ple_proj': nrm((DEPTH, PLE_DIM, D_MODEL), PLE_DIM ** -0.5),
    }


def reference(x_prompt, x_sample, p_prompt, p_sample, cache_mla_ckv, cache_mla_krope,
              cache_sb_k, cache_sb_v, cache_swa_k, cache_swa_v, state_ffn_conv,
              g_mix, g_ffn, g_ple, g_final,
              w_mla_dq, g_mla_q, w_mla_uq, w_mla_dkv, g_mla_kv, w_mla_uk, w_mla_uv, w_mla_o,
              w_sb_qkv, w_sb_o, w_swa_qkv, b_swa_qkv, swa_sinks, w_swa_o,
              w_ffn_in, ffn_conv_w, ffn_conv_b, w_ffn_out, w_ple_gate, w_ple_proj):
    xp, xs = x_prompt, x_sample
    mla_ckv_p, mla_kr_p, mla_ckv_s, mla_kr_s = [], [], [], []
    sb_k_p, sb_v_p, sb_k_s, sb_v_s = [], [], [], []
    swa_k_p, swa_v_p, swa_k_s, swa_v_s = [], [], [], []
    conv_p, conv_s = [], []
    for i in range(DEPTH):
        j = i // N_MIXERS
        hp = rms_norm(xp, g_mix[i])
        hs = rms_norm(xs, g_mix[i])
        if i % N_MIXERS == 0:
            mp, ms, a0, a1, a2, a3 = mla_mixer(hp, hs, cache_mla_ckv[j], cache_mla_krope[j],
                                               w_mla_dq[j], g_mla_q[j], w_mla_uq[j], w_mla_dkv[j],
                                               g_mla_kv[j], w_mla_uk[j], w_mla_uv[j], w_mla_o[j])
            mla_ckv_p.append(a0); mla_kr_p.append(a1); mla_ckv_s.append(a2); mla_kr_s.append(a3)
        elif i % N_MIXERS == 1:
            mp, ms, a0, a1, a2, a3 = sb_mixer(hp, hs, cache_sb_k[j], cache_sb_v[j], w_sb_qkv[j], w_sb_o[j])
            sb_k_p.append(a0); sb_v_p.append(a1); sb_k_s.append(a2); sb_v_s.append(a3)
        else:
            mp, ms, a0, a1, a2, a3 = swa_mixer(hp, hs, cache_swa_k[j], cache_swa_v[j],
                                               w_swa_qkv[j], b_swa_qkv[j], swa_sinks[j], w_swa_o[j])
            swa_k_p.append(a0); swa_v_p.append(a1); swa_k_s.append(a2); swa_v_s.append(a3)
        xp = xp + mp
        xs = xs + ms
        zero_state = jnp.zeros((xp.shape[0], CONV_W - 1, D_FF), xp.dtype)
        fp, cp = conv_ffn(rms_norm(xp, g_ffn[i]), zero_state, w_ffn_in[i], ffn_conv_w[i], ffn_conv_b[i], w_ffn_out[i])
        fs, cs = conv_ffn(rms_norm(xs, g_ffn[i]), state_ffn_conv[i], w_ffn_in[i], ffn_conv_w[i], ffn_conv_b[i], w_ffn_out[i])
        conv_p.append(cp); conv_s.append(cs)
        xp = xp + fp
        xs = xs + fs
        xp = xp + ple(xp, p_prompt[i], g_ple[i], w_ple_gate[i], w_ple_proj[i])
        xs = xs + ple(xs, p_sample[i], g_ple[i], w_ple_gate[i], w_ple_proj[i])
    y_prompt = rms_norm(xp, g_final)
    y_sample = rms_norm(xs, g_final)
    return (y_prompt, y_sample,
            jnp.stack(mla_ckv_p), jnp.stack(mla_kr_p), jnp.stack(mla_ckv_s), jnp.stack(mla_kr_s),
            jnp.stack(sb_k_p), jnp.stack(sb_v_p), jnp.stack(sb_k_s), jnp.stack(sb_v_s),
            jnp.stack(swa_k_p), jnp.stack(swa_v_p), jnp.stack(swa_k_s), jnp.stack(swa_v_s),
            jnp.stack(conv_p), jnp.stack(conv_s))
```

```python
import functools

import jax
import jax.numpy as jnp
from jax import lax
from jax.experimental import pallas as pl
from jax.experimental.pallas import tpu as pltpu

F32 = jnp.float32
BF16 = jnp.bfloat16

D_MODEL = 1024
CHUNK = 64
N_MIXERS = 3
RMS_EPS = 1e-6
ROPE_THETA = 10000.0
NEG = -1e30

MLA_HEADS = 16
MLA_NOPE = 128
MLA_ROPE = 64
MLA_V = 128
MLA_Q_LORA = 384
MLA_KV_LORA = 256
MLA_SCALE = (MLA_NOPE + MLA_ROPE) ** -0.5
MLA_QK = MLA_KV_LORA + 128

SB_HEADS = 16
SB_HEAD_DIM = 64
SB_SCALE = SB_HEAD_DIM ** -0.5

SWA_HEADS = 16
SWA_KV_HEADS = 4
SWA_GROUP = SWA_HEADS // SWA_KV_HEADS
SWA_HEAD_DIM = 64
SWA_WINDOW = 128
SWA_SCALE = SWA_HEAD_DIM ** -0.5
SWA_PERM = tuple(8 * p + 4 * s + g for p in range(2) for g in range(SWA_GROUP) for s in range(2))

D_FF = 2816
FF_CHUNK = 256
N_FF_CHUNKS = D_FF // FF_CHUNK
PLE_DIM = 256

LANES = 128
VMEM_LIMIT = 56 * 1024 * 1024


def _dot(a, b):
    return jnp.dot(a, b, preferred_element_type=F32)


def _dot_nt(a, b):
    return lax.dot_general(a, b, (((1,), (1,)), ((), ())), preferred_element_type=F32)


def _rms(x, g):
    ms = jnp.mean(x * x, axis=-1, keepdims=True)
    return x * lax.rsqrt(ms + RMS_EPS) * g


def _const_spec(shape):
    nd = len(shape)
    return pl.BlockSpec(shape, lambda *_: (0,) * nd, pipeline_mode=pl.Buffered(1))


def _params(sem):
    return pltpu.CompilerParams(dimension_semantics=sem, vmem_limit_bytes=VMEM_LIMIT)


def _mla_proj_kernel(x_ref, g_ref, wh_ref, gq_ref, wuq_ref, wuk_ref, gkv_ref, cos_ref, sin_ref,
                     q_ref, ckv_ref, kr_ref, kcat_ref):
    h = _rms(x_ref[...], g_ref[...]).astype(BF16)
    y = _dot(h, wh_ref[...])
    cq = _rms(y[:, :MLA_Q_LORA], gq_ref[...]).astype(BF16)
    o = MLA_Q_LORA
    ckv = _rms(y[:, o:o + MLA_KV_LORA], gkv_ref[...])
    cos = cos_ref[...]
    sin = sin_ref[...]
    o += MLA_KV_LORA
    kr = y[:, o:o + LANES] * cos + y[:, o + LANES:o + 2 * LANES] * sin
    ckv_ref[...] = ckv
    kr_ref[...] = kr[:, :MLA_ROPE]
    kcat_ref[...] = jnp.concatenate([ckv, kr], axis=-1).astype(BF16)
    qa = _dot(cq, wuq_ref[...])
    nh = MLA_HEADS * LANES
    for hd in range(MLA_HEADS):
        sl = slice(hd * LANES, (hd + 1) * LANES)
        ql = _dot(qa[:, sl].astype(BF16), wuk_ref[hd])
        qr = qa[:, nh:2 * nh][:, sl] * cos + qa[:, 2 * nh:][:, sl] * sin
        q_ref[hd] = (jnp.concatenate([ql, qr], axis=-1) * MLA_SCALE).astype(BF16)


def _mla_project(x2d, g, w, cos, sin, tm):
    n = x2d.shape[0]
    ntab = cos.shape[0] // tm
    row = lambda i: (i, 0)
    tab = lambda i: (i % ntab, 0)
    return pl.pallas_call(
        _mla_proj_kernel,
        grid=(n // tm,),
        in_specs=[
            pl.BlockSpec((tm, D_MODEL), row),
            _const_spec((1, D_MODEL)),
            _const_spec(w["wh"].shape),
            _const_spec((1, MLA_Q_LORA)),
            _const_spec(w["wuq"].shape),
            _const_spec(w["wuk"].shape),
            _const_spec((1, MLA_KV_LORA)),
            pl.BlockSpec((tm, LANES), tab),
            pl.BlockSpec((tm, LANES), tab),
        ],
        out_specs=[
            pl.BlockSpec((MLA_HEADS, tm, MLA_QK), lambda i: (0, i, 0)),
            pl.BlockSpec((tm, MLA_KV_LORA), row),
            pl.BlockSpec((tm, MLA_ROPE), row),
            pl.BlockSpec((tm, MLA_QK), row),
        ],
        out_shape=[
            jax.ShapeDtypeStruct((MLA_HEADS, n, MLA_QK), BF16),
            jax.ShapeDtypeStruct((n, MLA_KV_LORA), F32),
            jax.ShapeDtypeStruct((n, MLA_ROPE), F32),
            jax.ShapeDtypeStruct((n, MLA_QK), BF16),
        ],
        compiler_params=_params(("parallel",)),
        name="mla_proj",
    )(x2d, g, w["wh"], w["gq"], w["wuq"], w["wuk"], w["gkv"], cos, sin)


def _sb_proj_kernel(x_ref, g_ref, w_ref, qkv_ref, k_ref, v_ref):
    h = _rms(x_ref[...], g_ref[...]).astype(BF16)
    y = _dot(h, w_ref[...])
    k_ref[...] = y[:, D_MODEL:2 * D_MODEL]
    v_ref[...] = y[:, 2 * D_MODEL:]
    qkv_ref[:, :D_MODEL] = (y[:, :D_MODEL] * SB_SCALE).astype(BF16)
    qkv_ref[:, D_MODEL:] = y[:, D_MODEL:].astype(BF16)


def _sb_project(x2d, g, w, tm):
    n = x2d.shape[0]
    row = lambda i: (i, 0)
    return pl.pallas_call(
        _sb_proj_kernel,
        grid=(n // tm,),
        in_specs=[pl.BlockSpec((tm, D_MODEL), row), _const_spec((1, D_MODEL)), _const_spec(w.shape)],
        out_specs=[pl.BlockSpec((tm, 3 * D_MODEL), row), pl.BlockSpec((tm, D_MODEL), row),
                   pl.BlockSpec((tm, D_MODEL), row)],
        out_shape=[jax.ShapeDtypeStruct((n, 3 * D_MODEL), BF16), jax.ShapeDtypeStruct((n, D_MODEL), F32),
                   jax.ShapeDtypeStruct((n, D_MODEL), F32)],
        compiler_params=_params(("parallel",)),
        name="sb_proj",
    )(x2d, g, w)


def _swa_proj_kernel(x_ref, g_ref, w_ref, b_ref, cos_ref, sin_ref, q_ref, k_ref, v_ref, kb_ref, vb_ref):
    h = _rms(x_ref[...], g_ref[...]).astype(BF16)
    y = _dot(h, w_ref[...]) + b_ref[...]
    cos = cos_ref[...]
    sin = sin_ref[...]
    nq = SWA_HEADS * SWA_HEAD_DIM
    nk = SWA_KV_HEADS * SWA_HEAD_DIM
    for c in range(nq // LANES):
        sl = slice(c * LANES, (c + 1) * LANES)
        q = y[:, :nq][:, sl] * cos + y[:, nq:2 * nq][:, sl] * sin
        q_ref[:, sl] = (q * SWA_SCALE).astype(BF16)
    o = 2 * nq
    for c in range(nk // LANES):
        sl = slice(c * LANES, (c + 1) * LANES)
        k = y[:, o:o + nk][:, sl] * cos + y[:, o + nk:o + 2 * nk][:, sl] * sin
        k_ref[:, sl] = k
        kb_ref[:, sl] = k.astype(BF16)
    v = y[:, o + 2 * nk:]
    v_ref[...] = v
    vb_ref[...] = v.astype(BF16)


def _swa_project(x2d, g, w, b, cos, sin, tm):
    n = x2d.shape[0]
    ntab = cos.shape[0] // tm
    nq = SWA_HEADS * SWA_HEAD_DIM
    nk = SWA_KV_HEADS * SWA_HEAD_DIM
    row = lambda i: (i, 0)
    tab = lambda i: (i % ntab, 0)
    return pl.pallas_call(
        _swa_proj_kernel,
        grid=(n // tm,),
        in_specs=[pl.BlockSpec((tm, D_MODEL), row), _const_spec((1, D_MODEL)), _const_spec(w.shape),
                  _const_spec(b.shape), pl.BlockSpec((tm, LANES), tab), pl.BlockSpec((tm, LANES), tab)],
        out_specs=[pl.BlockSpec((tm, nq), row), pl.BlockSpec((tm, nk), row), pl.BlockSpec((tm, nk), row),
                   pl.BlockSpec((tm, nk), row), pl.BlockSpec((tm, nk), row)],
        out_shape=[jax.ShapeDtypeStruct((n, nq), BF16), jax.ShapeDtypeStruct((n, nk), F32),
                   jax.ShapeDtypeStruct((n, nk), F32), jax.ShapeDtypeStruct((n, nk), BF16),
                   jax.ShapeDtypeStruct((n, nk), BF16)],
        compiler_params=_params(("parallel",)),
        name="swa_proj",
    )(x2d, g, w, b, cos, sin)


def _softmax_step(s, v, m_sc, l_sc, acc_sc):
    m_old = m_sc[...]
    m_new = jnp.maximum(m_old, jnp.max(s, axis=-1, keepdims=True))
    alpha = jnp.exp(m_old - m_new)
    p = jnp.exp(s - m_new)
    l_sc[...] = alpha * l_sc[...] + jnp.sum(p, axis=-1, keepdims=True)
    acc_sc[...] = alpha * acc_sc[...] + _dot(p.astype(BF16), v)
    m_sc[...] = m_new


def _mla_finish(wuv_ref, o_ref, l_sc, acc_sc, tq):
    o_lat = (acc_sc[...] / l_sc[...]).astype(BF16)
    for hd in range(MLA_HEADS):
        o = _dot(o_lat[hd * tq:(hd + 1) * tq], wuv_ref[hd])
        o_ref[:, hd * MLA_V:(hd + 1) * MLA_V] = o.astype(BF16)


def _mla_attn_kernel(q_ref, k_ref, wuv_ref, o_ref, m_sc, l_sc, acc_sc, *, tq, tk):
    qi = pl.program_id(1)
    rows = MLA_HEADS * tq
    q = q_ref[...].reshape(rows, MLA_QK)
    m_sc[...] = jnp.full(m_sc.shape, NEG, F32)
    l_sc[...] = jnp.zeros(l_sc.shape, F32)
    acc_sc[...] = jnp.zeros(acc_sc.shape, F32)
    tok = lax.broadcasted_iota(jnp.int32, (rows, 1), 0) & (tq - 1)
    q_chunk = (qi * tq + tok) // CHUNK
    k_iota = lax.broadcasted_iota(jnp.int32, (1, tk), 1)

    def body(j, _):
        k = k_ref[0, pl.ds(pl.multiple_of(j * tk, tk), tk), :]
        s = _dot_nt(q, k)
        k_chunk = (j * tk + k_iota) // CHUNK
        s = jnp.where(k_chunk <= q_chunk, s, NEG)
        _softmax_step(s, k[:, :MLA_KV_LORA], m_sc, l_sc, acc_sc)
        return 0

    lax.fori_loop(0, (qi * tq + tq + tk - 1) // tk, body, 0)
    _mla_finish(wuv_ref, o_ref, l_sc, acc_sc, tq)


def _mla_attention(q, kcat, wuv, b, s, tq, tk):
    n = b * s
    nq = s // tq
    rows = MLA_HEADS * tq
    return pl.pallas_call(
        functools.partial(_mla_attn_kernel, tq=tq, tk=tk),
        grid=(b, nq),
        in_specs=[
            pl.BlockSpec((MLA_HEADS, tq, MLA_QK), lambda bi, qi: (0, bi * nq + qi, 0)),
            pl.BlockSpec((1, s, MLA_QK), lambda bi, qi: (bi, 0, 0)),
            _const_spec(wuv.shape),
        ],
        out_specs=pl.BlockSpec((tq, MLA_HEADS * MLA_V), lambda bi, qi: (bi * nq + qi, 0)),
        out_shape=jax.ShapeDtypeStruct((n, MLA_HEADS * MLA_V), BF16),
        scratch_shapes=[pltpu.VMEM((rows, 1), F32), pltpu.VMEM((rows, 1), F32),
                        pltpu.VMEM((rows, MLA_KV_LORA), F32)],
        compiler_params=_params(("parallel", "arbitrary")),
        name="mla_attn",
    )(q, kcat.reshape(b, s, MLA_QK), wuv)


def _mla_attn_sample_kernel(q_ref, cc_ref, cr_ref, kn_ref, wuv_ref, o_ref, m_sc, l_sc, acc_sc, *, t, tk, past):
    rows = MLA_HEADS * t
    q = q_ref[...].reshape(rows, MLA_QK)
    q_lat = q[:, :MLA_KV_LORA]
    q_rope = q[:, MLA_KV_LORA:MLA_KV_LORA + MLA_ROPE]
    m_sc[...] = jnp.full(m_sc.shape, NEG, F32)
    l_sc[...] = jnp.zeros(l_sc.shape, F32)
    acc_sc[...] = jnp.zeros(acc_sc.shape, F32)

    def body(j, _):
        sl = pl.ds(pl.multiple_of(j * tk, tk), tk)
        ck = cc_ref[0, sl, :].astype(BF16)
        s = _dot_nt(q_lat, ck) + _dot_nt(q_rope, cr_ref[0, sl, :].astype(BF16))
        _softmax_step(s, ck, m_sc, l_sc, acc_sc)
        return 0

    lax.fori_loop(0, past // tk, body, 0)
    kn = kn_ref[...]
    s = _dot_nt(q, kn)
    tok = lax.broadcasted_iota(jnp.int32, (rows, 1), 0) & (t - 1)
    k_chunk = (past + lax.broadcasted_iota(jnp.int32, (1, t), 1)) // CHUNK
    s = jnp.where(k_chunk <= (past + tok) // CHUNK, s, NEG)
    _softmax_step(s, kn[:, :MLA_KV_LORA], m_sc, l_sc, acc_sc)
    _mla_finish(wuv_ref, o_ref, l_sc, acc_sc, t)


def _mla_attention_sample(q, cache_ckv, cache_kr, kcat, wuv, layer, tk):
    _, db, past, _ = cache_ckv.shape
    t = q.shape[1] // db
    rows = MLA_HEADS * t
    return pl.pallas_call(
        functools.partial(_mla_attn_sample_kernel, t=t, tk=tk, past=past),
        grid=(db,),
        in_specs=[
            pl.BlockSpec((MLA_HEADS, t, MLA_QK), lambda bi: (0, bi, 0)),
            pl.BlockSpec((None, 1, past, MLA_KV_LORA), lambda bi: (layer, bi, 0, 0)),
            pl.BlockSpec((None, 1, past, MLA_ROPE), lambda bi: (layer, bi, 0, 0)),
            pl.BlockSpec((t, MLA_QK), lambda bi: (bi, 0)),
            _const_spec(wuv.shape),
        ],
        out_specs=pl.BlockSpec((t, MLA_HEADS * MLA_V), lambda bi: (bi, 0)),
        out_shape=jax.ShapeDtypeStruct((db * t, MLA_HEADS * MLA_V), BF16),
        scratch_shapes=[pltpu.VMEM((rows, 1), F32), pltpu.VMEM((rows, 1), F32),
                        pltpu.VMEM((rows, MLA_KV_LORA), F32)],
        compiler_params=_params(("parallel",)),
        name="mla_attn_sample",
    )(q, cache_ckv, cache_kr, kcat, wuv)


def _later_key_matrix(tk):
    r = lax.broadcasted_iota(jnp.int32, (tk, tk), 0)
    c = lax.broadcasted_iota(jnp.int32, (tk, tk), 1)
    return jnp.where(r > c, 1.0, 0.0).astype(BF16)


def _sb_block(qh, k2, v2, carry, acc, u, mask):
    z = _dot_nt(qh, k2)
    log_sig = jnp.minimum(z, 0.0) - jnp.log1p(jnp.exp(-jnp.abs(z)))
    log_not = log_sig - z
    if mask is not None:
        log_not = jnp.where(mask, log_not, 0.0)
    hi = log_not.astype(BF16)
    lo = (log_not - hi.astype(F32)).astype(BF16)
    tail = _dot(hi, u) + _dot(lo, u) + carry
    a = jnp.exp(log_sig + tail)
    if mask is not None:
        a = jnp.where(mask, a, 0.0)
    acc = acc + _dot(a.astype(BF16), v2)
    carry = carry + jnp.sum(log_not, axis=-1, keepdims=True)
    return carry, acc


def _split_pair(q2):
    lane = lax.broadcasted_iota(jnp.int32, (1, LANES), 1)
    zero = jnp.zeros_like(q2)
    return jnp.where(lane < LANES // 2, q2, zero), jnp.where(lane >= LANES // 2, q2, zero), lane


def _sb_attn_kernel(q_ref, k_ref, v_ref, o_ref, *, t):
    qi = pl.program_id(2)
    q_lo, q_hi, lane = _split_pair(q_ref[...])
    u = _later_key_matrix(t)
    r = lax.broadcasted_iota(jnp.int32, (t, t), 0)
    c = lax.broadcasted_iota(jnp.int32, (t, t), 1)
    diag_mask = c < r
    zc = jnp.zeros((t, 1), F32)
    za = jnp.zeros((t, LANES), F32)

    def block(j, state, mask):
        sl = pl.ds(pl.multiple_of(j * t, t), t)
        k2 = k_ref[0, sl, :]
        v2 = v_ref[0, sl, :]
        c_lo, a_lo, c_hi, a_hi = state
        c_lo, a_lo = _sb_block(q_lo, k2, v2, c_lo, a_lo, u, mask)
        c_hi, a_hi = _sb_block(q_hi, k2, v2, c_hi, a_hi, u, mask)
        return c_lo, a_lo, c_hi, a_hi

    state = block(qi, (zc, za, zc, za), diag_mask)
    state = lax.fori_loop(0, qi, lambda jj, st: block(qi - 1 - jj, st, None), state)
    o_ref[...] = jnp.where(lane < LANES // 2, state[1], state[3]).astype(BF16)


def _sb_attention(qkv, b, s, t):
    n = b * s
    nq = s // t
    npair = SB_HEADS // 2
    qkv3 = qkv.reshape(b, s, 3 * D_MODEL)
    return pl.pallas_call(
        functools.partial(_sb_attn_kernel, t=t),
        grid=(b, npair, nq),
        in_specs=[
            pl.BlockSpec((t, LANES), lambda bi, g, qi: (bi * nq + qi, g)),
            pl.BlockSpec((1, s, LANES), lambda bi, g, qi: (bi, 0, npair + g)),
            pl.BlockSpec((1, s, LANES), lambda bi, g, qi: (bi, 0, 2 * npair + g)),
        ],
        out_specs=pl.BlockSpec((t, LANES), lambda bi, g, qi: (bi * nq + qi, g)),
        out_shape=jax.ShapeDtypeStruct((n, D_MODEL), BF16),
        compiler_params=_params(("parallel", "parallel", "arbitrary")),
        name="sb_attn",
    )(qkv, qkv3, qkv3)


def _sb_attn_sample_kernel(q_ref, kn_ref, vn_ref, ck_ref, cv_ref, o_ref, *, t, tk, past):
    q_lo, q_hi, lane = _split_pair(q_ref[...])
    r = lax.broadcasted_iota(jnp.int32, (t, t), 0)
    c = lax.broadcasted_iota(jnp.int32, (t, t), 1)
    zc = jnp.zeros((t, 1), F32)
    za = jnp.zeros((t, LANES), F32)
    u_new = _later_key_matrix(t)
    kn = kn_ref[...]
    vn = vn_ref[...]
    c_lo, a_lo = _sb_block(q_lo, kn, vn, zc, za, u_new, c < r)
    c_hi, a_hi = _sb_block(q_hi, kn, vn, zc, za, u_new, c < r)
    u = _later_key_matrix(tk)

    def body(jj, state):
        j = past // tk - 1 - jj
        sl = pl.ds(pl.multiple_of(j * tk, tk), tk)
        k2 = ck_ref[0, sl, :].astype(BF16)
        v2 = cv_ref[0, sl, :].astype(BF16)
        c_lo, a_lo, c_hi, a_hi = state
        c_lo, a_lo = _sb_block(q_lo, k2, v2, c_lo, a_lo, u, None)
        c_hi, a_hi = _sb_block(q_hi, k2, v2, c_hi, a_hi, u, None)
        return c_lo, a_lo, c_hi, a_hi

    state = lax.fori_loop(0, past // tk, body, (c_lo, a_lo, c_hi, a_hi))
    o_ref[...] = jnp.where(lane < LANES // 2, state[1], state[3]).astype(BF16)


def _sb_attention_sample(qkv, cache_k, cache_v, layer, db, tk):
    n = qkv.shape[0]
    t = n // db
    past = cache_k.shape[2]
    npair = SB_HEADS // 2
    ck = cache_k.reshape(cache_k.shape[0], db, past, D_MODEL)
    cv = cache_v.reshape(cache_v.shape[0], db, past, D_MODEL)
    return pl.pallas_call(
        functools.partial(_sb_attn_sample_kernel, t=t, tk=tk, past=past),
        grid=(db, npair),
        in_specs=[
            pl.BlockSpec((t, LANES), lambda bi, g: (bi, g)),
            pl.BlockSpec((t, LANES), lambda bi, g: (bi, npair + g)),
            pl.BlockSpec((t, LANES), lambda bi, g: (bi, 2 * npair + g)),
            pl.BlockSpec((None, 1, past, LANES), lambda bi, g: (layer, bi, 0, g)),
            pl.BlockSpec((None, 1, past, LANES), lambda bi, g: (layer, bi, 0, g)),
        ],
        out_specs=pl.BlockSpec((t, LANES), lambda bi, g: (bi, g)),
        out_shape=jax.ShapeDtypeStruct((n, D_MODEL), BF16),
        compiler_params=_params(("parallel", "parallel")),
        name="sb_attn_sample",
    )(qkv, qkv, qkv, ck, cv)


def _swa_heads(sink_ref, q_ref, kband, vband, mask, o_ref):
    for blk in range(SWA_HEADS // 2):
        pair = blk // SWA_GROUP
        k2 = kband[:, pair * LANES:(pair + 1) * LANES]
        v2 = vband[:, pair * LANES:(pair + 1) * LANES]
        q_lo, q_hi, lane = _split_pair(q_ref[:, blk * LANES:(blk + 1) * LANES])
        outs = []
        for half, qh in enumerate((q_lo, q_hi)):
            s = jnp.where(mask, _dot_nt(qh, k2), NEG)
            sink = sink_ref[2 * blk + half]
            m = jnp.maximum(jnp.max(s, axis=-1, keepdims=True), sink)
            e = jnp.exp(s - m)
            den = jnp.sum(e, axis=-1, keepdims=True) + jnp.exp(sink - m)
            outs.append(_dot((e / den).astype(BF16), v2))
        o_ref[:, blk * LANES:(blk + 1) * LANES] = jnp.where(lane < LANES // 2, outs[0], outs[1]).astype(BF16)


def _swa_attn_kernel(sink_ref, q_ref, kp_ref, kc_ref, vp_ref, vc_ref, o_ref, *, tq):
    i = pl.program_id(1)
    kband = jnp.concatenate([kp_ref[0], kc_ref[0]], axis=0)
    vband = jnp.concatenate([vp_ref[0], vc_ref[0]], axis=0)
    q_pos = i * tq + lax.broadcasted_iota(jnp.int32, (tq, 1), 0)
    k_pos = i * tq - SWA_WINDOW + lax.broadcasted_iota(jnp.int32, (1, SWA_WINDOW + tq), 1)
    dc = (q_pos + SWA_WINDOW) // CHUNK - (k_pos + SWA_WINDOW) // CHUNK
    mask = (dc >= 0) & (dc <= SWA_WINDOW // CHUNK) & (k_pos >= 0)
    _swa_heads(sink_ref, q_ref, kband, vband, mask, o_ref)


def _swa_attention(sinks, q, kb, vb, b, s):
    tq = SWA_WINDOW
    nq = s // tq
    nk = SWA_KV_HEADS * SWA_HEAD_DIM
    kb3 = kb.reshape(b, s, nk)
    vb3 = vb.reshape(b, s, nk)
    prev = lambda bi, qi: (bi, jnp.maximum(qi - 1, 0), 0)
    cur = lambda bi, qi: (bi, qi, 0)
    return pl.pallas_call(
        functools.partial(_swa_attn_kernel, tq=tq),
        grid=(b, nq),
        in_specs=[
            pl.BlockSpec(memory_space=pltpu.SMEM),
            pl.BlockSpec((tq, D_MODEL), lambda bi, qi: (bi * nq + qi, 0)),
            pl.BlockSpec((1, tq, nk), prev), pl.BlockSpec((1, tq, nk), cur),
            pl.BlockSpec((1, tq, nk), prev), pl.BlockSpec((1, tq, nk), cur),
        ],
        out_specs=pl.BlockSpec((tq, D_MODEL), lambda bi, qi: (bi * nq + qi, 0)),
        out_shape=jax.ShapeDtypeStruct((b * s, D_MODEL), BF16),
        compiler_params=_params(("parallel", "arbitrary")),
        name="swa_attn",
    )(sinks, q, kb3, kb3, vb3, vb3)


def _swa_attn_sample_kernel(sink_ref, q_ref, ck_ref, cv_ref, kn_ref, vn_ref, o_ref, *, t, past):
    clen = ck_ref.shape[1]
    kband = jnp.concatenate([ck_ref[0].astype(BF16), kn_ref[...]], axis=0)
    vband = jnp.concatenate([cv_ref[0].astype(BF16), vn_ref[...]], axis=0)
    q_pos = past + lax.broadcasted_iota(jnp.int32, (t, 1), 0)
    k_pos = past - clen + lax.broadcasted_iota(jnp.int32, (1, clen + t), 1)
    dc = q_pos // CHUNK - (k_pos + SWA_WINDOW) // CHUNK + SWA_WINDOW // CHUNK
    mask = (dc >= 0) & (dc <= SWA_WINDOW // CHUNK) & (k_pos >= 0)
    _swa_heads(sink_ref, q_ref, kband, vband, mask, o_ref)


def _swa_attention_sample(sinks, q, cache_k, cache_v, kb, vb, layer, db, past):
    n = q.shape[0]
    t = n // db
    clen = cache_k.shape[2]
    nk = SWA_KV_HEADS * SWA_HEAD_DIM
    ck = cache_k.reshape(cache_k.shape[0], db, clen, nk)
    cv = cache_v.reshape(cache_v.shape[0], db, clen, nk)
    row = lambda bi: (bi, 0)
    cache = lambda bi: (layer, bi, 0, 0)
    return pl.pallas_call(
        functools.partial(_swa_attn_sample_kernel, t=t, past=past),
        grid=(db,),
        in_specs=[
            pl.BlockSpec(memory_space=pltpu.SMEM),
            pl.BlockSpec((t, D_MODEL), row),
            pl.BlockSpec((None, 1, clen, nk), cache), pl.BlockSpec((None, 1, clen, nk), cache),
            pl.BlockSpec((t, nk), row), pl.BlockSpec((t, nk), row),
        ],
        out_specs=pl.BlockSpec((t, D_MODEL), row),
        out_shape=jax.ShapeDtypeStruct((n, D_MODEL), BF16),
        compiler_params=_params(("parallel",)),
        name="swa_attn_sample",
    )(sinks, q, ck, cv, kb, vb)


def _post_kernel(x_ref, o_ref, wo_ref, gf_ref, wg_ref, wu_ref, cw_ref, cb_ref, wout_ref, st_ref,
                 p_ref, gp_ref, wpg_ref, wpp_ref, gfin_ref, y_ref, cs_ref, carry_sc, *, final):
    ti = pl.program_id(1)
    ns, tt, _ = x_ref.shape
    rows = ns * tt
    x = x_ref[...].reshape(rows, D_MODEL)
    x = x + _dot(o_ref[...].reshape(rows, o_ref.shape[-1]), wo_ref[...])
    h = _rms(x, gf_ref[...]).astype(BF16)
    t_idx = lax.broadcasted_iota(jnp.int32, (1, tt, 1), 1)

    @pl.when(ti == 0)
    def _():
        carry_sc[...] = st_ref[...]

    acc = jnp.zeros((rows, D_MODEL), F32)
    for c in range(N_FF_CHUNKS):
        sl = slice(c * FF_CHUNK, (c + 1) * FF_CHUNK)
        gate = _dot(h, wg_ref[c])
        up = _dot(h, wu_ref[c])
        st = carry_sc[:, :, sl]
        g1 = pltpu.roll(gate, 1, 0).reshape(ns, tt, FF_CHUNK)
        g2 = pltpu.roll(gate, 2, 0).reshape(ns, tt, FF_CHUNK)
        gate = gate.reshape(ns, tt, FF_CHUNK)
        g1 = jnp.where(t_idx == 0, st[:, 1:2, :], g1)
        g2 = jnp.where(t_idx == 0, st[:, 0:1, :], jnp.where(t_idx == 1, st[:, 1:2, :], g2))
        cw = cw_ref[:, sl]
        conv = g2 * cw[0:1, :] + g1 * cw[1:2, :] + gate * cw[2:3, :] + cb_ref[:, sl]
        new_st = gate[:, tt - 2:, :]
        carry_sc[:, :, sl] = new_st
        cs_ref[:, :, sl] = new_st
        act = 0.5 * conv * (1.0 + lax.erf(conv * (0.5 ** 0.5)))
        y = (act * up.reshape(ns, tt, FF_CHUNK)).reshape(rows, FF_CHUNK)
        acc = acc + _dot(y.astype(BF16), wout_ref[c])
    x = x + acc
    hp = _rms(x, gp_ref[...]).astype(BF16)
    gate_p = jax.nn.sigmoid(_dot(hp, wpg_ref[...]))
    proj = _dot(p_ref[...].reshape(rows, PLE_DIM).astype(BF16), wpp_ref[...])
    x = x + gate_p * proj
    if final:
        x = _rms(x, gfin_ref[...])
    y_ref[...] = x.reshape(ns, tt, D_MODEL)


def _post(x3, o2d, w, state, p4, layer, g_final, ns, tt, final):
    bx, tx, _ = x3.shape
    do = o2d.shape[-1]
    o3 = o2d.reshape(bx, tx, do)
    blk = lambda bi, ti: (bi, ti, 0)
    return pl.pallas_call(
        functools.partial(_post_kernel, final=final),
        grid=(bx // ns, tx // tt),
        in_specs=[
            pl.BlockSpec((ns, tt, D_MODEL), blk),
            pl.BlockSpec((ns, tt, do), blk),
            _const_spec(w["wo"].shape),
            _const_spec((1, D_MODEL)),
            _const_spec(w["wg"].shape),
            _const_spec(w["wu"].shape),
            _const_spec(w["cw"].shape),
            _const_spec(w["cb"].shape),
            _const_spec(w["wout"].shape),
            pl.BlockSpec((ns, 2, D_FF), lambda bi, ti: (bi, 0, 0)),
            pl.BlockSpec((None, ns, tt, PLE_DIM), lambda bi, ti: (layer, bi, ti, 0)),
            _const_spec((1, D_MODEL)),
            _const_spec(w["wpg"].shape),
            _const_spec(w["wpp"].shape),
            _const_spec((1, D_MODEL)),
        ],
        out_specs=[pl.BlockSpec((ns, tt, D_MODEL), blk),
                   pl.BlockSpec((ns, 2, D_FF), lambda bi, ti: (bi, 0, 0))],
        out_shape=[jax.ShapeDtypeStruct(x3.shape, F32), jax.ShapeDtypeStruct((bx, 2, D_FF), F32)],
        scratch_shapes=[pltpu.VMEM((ns, 2, D_FF), F32)],
        compiler_params=_params(("parallel", "arbitrary")),
        name="post_ffn",
    )(x3, o3, w["wo"], w["gf"], w["wg"], w["wu"], w["cw"], w["cb"], w["wout"], state, p4,
      w["gp"], w["wpg"], w["wpp"], g_final)


def _rot_cols(w):
    half = w.shape[-1] // 2
    return jnp.concatenate([-w[..., half:], w[..., :half]], axis=-1)


def _rope_tables(pos):
    half = MLA_ROPE // 2
    inv = ROPE_THETA ** (-jnp.arange(half, dtype=F32) / half)
    ang = pos.astype(F32)[:, None] * inv[None, :]
    reps = LANES // half
    return jnp.tile(jnp.cos(ang), (1, reps)), jnp.tile(jnp.sin(ang), (1, reps))


def _prep_mla(w_dq, g_q, w_uq, w_dkv, g_kv, w_uk, w_uv):
    z = jnp.zeros((D_MODEL, LANES - MLA_ROPE), F32)
    kr = w_dkv[:, MLA_KV_LORA:]
    wh = jnp.concatenate([w_dq, w_dkv[:, :MLA_KV_LORA], kr, z, _rot_cols(kr), z], axis=1)
    uq = w_uq.reshape(MLA_Q_LORA, MLA_HEADS, MLA_NOPE + MLA_ROPE)
    rp = uq[:, :, MLA_NOPE:]
    pad = jnp.zeros((MLA_Q_LORA, MLA_HEADS, LANES - MLA_ROPE), F32)
    nh = MLA_HEADS * LANES
    wuq = jnp.concatenate([uq[:, :, :MLA_NOPE].reshape(MLA_Q_LORA, nh),
                           jnp.concatenate([rp, pad], -1).reshape(MLA_Q_LORA, nh),
                           jnp.concatenate([_rot_cols(rp), pad], -1).reshape(MLA_Q_LORA, nh)], axis=1)
    return dict(wh=wh.astype(BF16), gq=g_q[None, :], wuq=wuq.astype(BF16), gkv=g_kv[None, :],
                wuk=jnp.transpose(w_uk, (1, 2, 0)).astype(BF16),
                wuv=jnp.transpose(w_uv, (1, 0, 2)).astype(BF16))


def _prep_swa(w_qkv, b_qkv, sinks, w_o):
    nq = SWA_HEADS * SWA_HEAD_DIM
    nk = SWA_KV_HEADS * SWA_HEAD_DIM
    perm = jnp.array(SWA_PERM)

    def split(a):
        lead = a.shape[:-1]
        q = a[..., :nq].reshape(*lead, SWA_HEADS, SWA_HEAD_DIM)[..., perm, :]
        k = a[..., nq:nq + nk].reshape(*lead, SWA_KV_HEADS, SWA_HEAD_DIM)
        return jnp.concatenate([q.reshape(*lead, nq), _rot_cols(q).reshape(*lead, nq),
                                k.reshape(*lead, nk), _rot_cols(k).reshape(*lead, nk), a[..., nq + nk:]], axis=-1)

    wo = w_o.reshape(SWA_HEADS, SWA_HEAD_DIM, D_MODEL)[perm].reshape(nq, D_MODEL)
    return split(w_qkv).astype(BF16), split(b_qkv)[None, :], sinks[perm], wo.astype(BF16)


def _prep_post(w_o, g_ffn, w_in, conv_w, conv_b, w_out, g_ple, w_gate, w_proj):
    def chunks(w):
        return jnp.transpose(w.reshape(D_MODEL, N_FF_CHUNKS, FF_CHUNK), (1, 0, 2)).astype(BF16)
    return dict(wo=w_o.astype(BF16), gf=g_ffn[None, :], wg=chunks(w_in[:, :D_FF]), wu=chunks(w_in[:, D_FF:]),
                cw=conv_w, cb=conv_b[None, :], wout=w_out.reshape(N_FF_CHUNKS, FF_CHUNK, D_MODEL).astype(BF16),
                gp=g_ple[None, :], wpg=w_gate.astype(BF16), wpp=w_proj.astype(BF16))


def kernel(x_prompt, x_sample, p_prompt, p_sample, cache_mla_ckv, cache_mla_krope, cache_sb_k, cache_sb_v, cache_swa_k, cache_swa_v, state_ffn_conv, g_mix, g_ffn, g_ple, g_final, w_mla_dq, g_mla_q, w_mla_uq, w_mla_dkv, g_mla_kv, w_mla_uk, w_mla_uv, w_mla_o, w_sb_qkv, w_sb_o, w_swa_qkv, b_swa_qkv, swa_sinks, w_swa_o, w_ffn_in, ffn_conv_w, ffn_conv_b, w_ffn_out, w_ple_gate, w_ple_proj):
    b, s, _ = x_prompt.shape
    db, t, _ = x_sample.shape
    depth = g_mix.shape[0]
    past = cache_mla_ckv.shape[2]
    n_p = b * s
    n_s = db * t
    assert s % 512 == 0 and t == CHUNK and past % 256 == 0 and n_s % 512 == 0
    assert cache_swa_k.shape[2] == SWA_WINDOW and 512 % t == 0

    tm_mla = 256
    tm = 512
    tt = 512
    ns = 512 // t
    cos_p, sin_p = _rope_tables(jnp.arange(s, dtype=jnp.int32))
    cos_s, sin_s = _rope_tables(past + (jnp.arange(tm, dtype=jnp.int32) % t))
    zero_state = jnp.zeros((b, 2, D_FF), F32)
    g_fin = g_final[None, :]

    xp, xs = x_prompt, x_sample
    outs = {k: [] for k in ("ckv_p", "kr_p", "ckv_s", "kr_s", "sbk_p", "sbv_p", "sbk_s", "sbv_s",
                            "swk_p", "swv_p", "swk_s", "swv_s", "conv_p", "conv_s")}
    for i in range(depth):
        j = i // N_MIXERS
        gm = g_mix[i][None, :]
        xp2 = xp.reshape(n_p, D_MODEL)
        xs2 = xs.reshape(n_s, D_MODEL)
        if i % N_MIXERS == 0:
            w = _prep_mla(w_mla_dq[j], g_mla_q[j], w_mla_uq[j], w_mla_dkv[j], g_mla_kv[j], w_mla_uk[j], w_mla_uv[j])
            q_p, ckv_p, kr_p, kcat_p = _mla_project(xp2, gm, w, cos_p, sin_p, tm_mla)
            q_s, ckv_s, kr_s, kcat_s = _mla_project(xs2, gm, w, cos_s[:tm_mla], sin_s[:tm_mla], tm_mla)
            o_p = _mla_attention(q_p, kcat_p, w["wuv"], b, s, 128, 256)
            o_s = _mla_attention_sample(q_s, cache_mla_ckv, cache_mla_krope, kcat_s, w["wuv"], j, 256)
            outs["ckv_p"].append(ckv_p.reshape(b, s, MLA_KV_LORA))
            outs["kr_p"].append(kr_p.reshape(b, s, MLA_ROPE))
            outs["ckv_s"].append(ckv_s.reshape(db, t, MLA_KV_LORA))
            outs["kr_s"].append(kr_s.reshape(db, t, MLA_ROPE))
            w_o = w_mla_o[j]
        elif i % N_MIXERS == 1:
            wq = w_sb_qkv[j].astype(BF16)
            qkv_p, k_p, v_p = _sb_project(xp2, gm, wq, tm)
            qkv_s, k_s, v_s = _sb_project(xs2, gm, wq, tm)
            o_p = _sb_attention(qkv_p, b, s, 256)
            o_s = _sb_attention_sample(qkv_s, cache_sb_k, cache_sb_v, j, db, 256)
            outs["sbk_p"].append(k_p.reshape(b, s, SB_HEADS, SB_HEAD_DIM))
            outs["sbv_p"].append(v_p.reshape(b, s, SB_HEADS, SB_HEAD_DIM))
            outs["sbk_s"].append(k_s.reshape(db, t, SB_HEADS, SB_HEAD_DIM))
            outs["sbv_s"].append(v_s.reshape(db, t, SB_HEADS, SB_HEAD_DIM))
            w_o = w_sb_o[j]
        else:
            wq, bq, sinks, w_o = _prep_swa(w_swa_qkv[j], b_swa_qkv[j], swa_sinks[j], w_swa_o[j])
            q_p, k_p, v_p, kb_p, vb_p = _swa_project(xp2, gm, wq, bq, cos_p, sin_p, tm)
            q_s, k_s, v_s, kb_s, vb_s = _swa_project(xs2, gm, wq, bq, cos_s, sin_s, tm)
            o_p = _swa_attention(sinks, q_p, kb_p, vb_p, b, s)
            o_s = _swa_attention_sample(sinks, q_s, cache_swa_k, cache_swa_v, kb_s, vb_s, j, db, past)
            kv_shape = (SWA_KV_HEADS, SWA_HEAD_DIM)
            outs["swk_p"].append(k_p.reshape(b, s, *kv_shape)[:, -SWA_WINDOW:])
            outs["swv_p"].append(v_p.reshape(b, s, *kv_shape)[:, -SWA_WINDOW:])
            outs["swk_s"].append(jnp.concatenate([cache_swa_k[j], k_s.reshape(db, t, *kv_shape)], 1)[:, -SWA_WINDOW:])
            outs["swv_s"].append(jnp.concatenate([cache_swa_v[j], v_s.reshape(db, t, *kv_shape)], 1)[:, -SWA_WINDOW:])
        wp = _prep_post(w_o, g_ffn[i], w_ffn_in[i], ffn_conv_w[i], ffn_conv_b[i], w_ffn_out[i],
                        g_ple[i], w_ple_gate[i], w_ple_proj[i])
        final = i == depth - 1
        xp, cp = _post(xp, o_p, wp, zero_state, p_prompt, i, g_fin, 1, tt, final)
        xs, cs = _post(xs, o_s, wp, state_ffn_conv[i], p_sample, i, g_fin, ns, t, final)
        outs["conv_p"].append(cp)
        outs["conv_s"].append(cs)
    order = ("ckv_p", "kr_p", "ckv_s", "kr_s", "sbk_p", "sbv_p", "sbk_s", "sbv_s",
             "swk_p", "swv_p", "swk_s", "swv_s", "conv_p", "conv_s")
    return (xp, xs) + tuple(jnp.stack(outs[k]) for k in order)
```

```python
import functools

import jax
import jax.numpy as jnp
from jax import lax
from jax.experimental import pallas as pl
from jax.experimental.pallas import tpu as pltpu

F32 = jnp.float32
BF16 = jnp.bfloat16

D_MODEL = 1024
CHUNK = 64
N_MIXERS = 3
RMS_EPS = 1e-6
ROPE_THETA = 10000.0
NEG = -1e30
LOG2E = 1.4426950408889634

MLA_HEADS = 16
MLA_NOPE = 128
MLA_ROPE = 64
MLA_V = 128
MLA_Q_LORA = 384
MLA_KV_LORA = 256
MLA_SCALE = (MLA_NOPE + MLA_ROPE) ** -0.5
MLA_QK = MLA_KV_LORA + 128

SB_HEADS = 16
SB_HEAD_DIM = 64
SB_SCALE = SB_HEAD_DIM ** -0.5

SWA_HEADS = 16
SWA_KV_HEADS = 4
SWA_GROUP = SWA_HEADS // SWA_KV_HEADS
SWA_HEAD_DIM = 64
SWA_WINDOW = 128
SWA_SCALE = SWA_HEAD_DIM ** -0.5
SWA_PERM = tuple(8 * p + 4 * s + g for p in range(2) for g in range(SWA_GROUP) for s in range(2))

D_FF = 2816
FF_CHUNK = 256
N_FF_CHUNKS = D_FF // FF_CHUNK
PLE_DIM = 256

LANES = 128
VMEM_LIMIT = 56 * 1024 * 1024


def _dot(a, b):
    return jnp.dot(a, b, preferred_element_type=F32)


def _dot_nt(a, b):
    return lax.dot_general(a, b, (((1,), (1,)), ((), ())), preferred_element_type=F32)


def _rms(x, g):
    ms = jnp.mean(x * x, axis=-1, keepdims=True)
    return x * lax.rsqrt(ms + RMS_EPS) * g


def _const_spec(shape):
    nd = len(shape)
    return pl.BlockSpec(shape, lambda *_: (0,) * nd, pipeline_mode=pl.Buffered(1))


def _params(sem):
    return pltpu.CompilerParams(dimension_semantics=sem, vmem_limit_bytes=VMEM_LIMIT)


def _mla_latents(x_ref, g_ref, wh_ref, gq_ref, gkv_ref, cos, sin, ckv_ref, kr_ref, kcat_ref):
    h = _rms(x_ref[...], g_ref[...]).astype(BF16)
    y = _dot(h, wh_ref[...])
    cq = _rms(y[:, :MLA_Q_LORA], gq_ref[...])
    o = MLA_Q_LORA
    ckv = _rms(y[:, o:o + MLA_KV_LORA], gkv_ref[...])
    o += MLA_KV_LORA
    kr = y[:, o:o + LANES] * cos + y[:, o + LANES:o + 2 * LANES] * sin
    ckv_ref[...] = ckv
    kr_ref[...] = kr[:, :MLA_ROPE]
    kcat_ref[...] = jnp.concatenate([ckv, kr], axis=-1).astype(BF16)
    return cq, ckv


def _mla_proj_t_kernel(x_ref, g_ref, wh_ref, gq_ref, wuq_ref, wuk_ref, gkv_ref, cos_ref, sin_ref,
                       cos_t_ref, sin_t_ref, q_ref, ckv_ref, kr_ref, kcat_ref, ckv_t_ref):
    cq, ckv = _mla_latents(x_ref, g_ref, wh_ref, gq_ref, gkv_ref, cos_ref[...], sin_ref[...],
                           ckv_ref, kr_ref, kcat_ref)
    ckv_t_ref[...] = ckv.T.astype(BF16)
    qa = _dot(wuq_ref[...], cq.T.astype(BF16))
    cos = cos_t_ref[...]
    sin = sin_t_ref[...]
    nh = MLA_HEADS * LANES
    for hd in range(MLA_HEADS):
        sl = slice(hd * LANES, (hd + 1) * LANES)
        ql = _dot(wuk_ref[hd], qa[sl].astype(BF16))
        qr = qa[nh:2 * nh][sl] * cos + qa[2 * nh:][sl] * sin
        q_ref[hd] = (jnp.concatenate([ql, qr], axis=0) * (MLA_SCALE * LOG2E)).astype(BF16)


def _mla_project_t(x2d, g, w, cos, sin, tm):
    n = x2d.shape[0]
    ntab = cos.shape[0] // tm
    row = lambda i: (i, 0)
    col = lambda i: (0, i)
    return pl.pallas_call(
        _mla_proj_t_kernel,
        grid=(n // tm,),
        in_specs=[
            pl.BlockSpec((tm, D_MODEL), row),
            _const_spec((1, D_MODEL)),
            _const_spec(w["wh"].shape),
            _const_spec((1, MLA_Q_LORA)),
            _const_spec(w["wuq_t"].shape),
            _const_spec(w["wuk_t"].shape),
            _const_spec((1, MLA_KV_LORA)),
            pl.BlockSpec((tm, LANES), lambda i: (i % ntab, 0)),
            pl.BlockSpec((tm, LANES), lambda i: (i % ntab, 0)),
            pl.BlockSpec((LANES, tm), lambda i: (0, i % ntab)),
            pl.BlockSpec((LANES, tm), lambda i: (0, i % ntab)),
        ],
        out_specs=[
            pl.BlockSpec((MLA_HEADS, MLA_QK, tm), lambda i: (0, 0, i)),
            pl.BlockSpec((tm, MLA_KV_LORA), row),
            pl.BlockSpec((tm, MLA_ROPE), row),
            pl.BlockSpec((tm, MLA_QK), row),
            pl.BlockSpec((MLA_KV_LORA, tm), col),
        ],
        out_shape=[
            jax.ShapeDtypeStruct((MLA_HEADS, MLA_QK, n), BF16),
            jax.ShapeDtypeStruct((n, MLA_KV_LORA), F32),
            jax.ShapeDtypeStruct((n, MLA_ROPE), F32),
            jax.ShapeDtypeStruct((n, MLA_QK), BF16),
            jax.ShapeDtypeStruct((MLA_KV_LORA, n), BF16),
        ],
        compiler_params=_params(("parallel",)),
        name="mla_proj_t",
    )(x2d, g, w["wh"], w["gq"], w["wuq_t"], w["wuk_t"], w["gkv"], cos, sin, cos.T, sin.T)


def _mla_proj_kernel(x_ref, g_ref, wh_ref, gq_ref, wuq_ref, wuk_ref, gkv_ref, cos_ref, sin_ref,
                     q_ref, ckv_ref, kr_ref, kcat_ref):
    cos = cos_ref[...]
    sin = sin_ref[...]
    cq, _ = _mla_latents(x_ref, g_ref, wh_ref, gq_ref, gkv_ref, cos, sin, ckv_ref, kr_ref, kcat_ref)
    qa = _dot(cq.astype(BF16), wuq_ref[...])
    nh = MLA_HEADS * LANES
    for hd in range(MLA_HEADS):
        sl = slice(hd * LANES, (hd + 1) * LANES)
        ql = _dot(qa[:, sl].astype(BF16), wuk_ref[hd])
        qr = qa[:, nh:2 * nh][:, sl] * cos + qa[:, 2 * nh:][:, sl] * sin
        q_ref[hd] = (jnp.concatenate([ql, qr], axis=-1) * MLA_SCALE).astype(BF16)


def _mla_project(x2d, g, w, cos, sin, tm):
    n = x2d.shape[0]
    ntab = cos.shape[0] // tm
    row = lambda i: (i, 0)
    tab = lambda i: (i % ntab, 0)
    return pl.pallas_call(
        _mla_proj_kernel,
        grid=(n // tm,),
        in_specs=[
            pl.BlockSpec((tm, D_MODEL), row),
            _const_spec((1, D_MODEL)),
            _const_spec(w["wh"].shape),
            _const_spec((1, MLA_Q_LORA)),
            _const_spec(w["wuq"].shape),
            _const_spec(w["wuk"].shape),
            _const_spec((1, MLA_KV_LORA)),
            pl.BlockSpec((tm, LANES), tab),
            pl.BlockSpec((tm, LANES), tab),
        ],
        out_specs=[
            pl.BlockSpec((MLA_HEADS, tm, MLA_QK), lambda i: (0, i, 0)),
            pl.BlockSpec((tm, MLA_KV_LORA), row),
            pl.BlockSpec((tm, MLA_ROPE), row),
            pl.BlockSpec((tm, MLA_QK), row),
        ],
        out_shape=[
            jax.ShapeDtypeStruct((MLA_HEADS, n, MLA_QK), BF16),
            jax.ShapeDtypeStruct((n, MLA_KV_LORA), F32),
            jax.ShapeDtypeStruct((n, MLA_ROPE), F32),
            jax.ShapeDtypeStruct((n, MLA_QK), BF16),
        ],
        compiler_params=_params(("parallel",)),
        name="mla_proj",
    )(x2d, g, w["wh"], w["gq"], w["wuq"], w["wuk"], w["gkv"], cos, sin)


def _sb_proj_kernel(x_ref, g_ref, w_ref, qkv_ref, k_ref, v_ref):
    h = _rms(x_ref[...], g_ref[...]).astype(BF16)
    y = _dot(h, w_ref[...])
    k_ref[...] = y[:, D_MODEL:2 * D_MODEL]
    v_ref[...] = y[:, 2 * D_MODEL:]
    qkv_ref[:, :D_MODEL] = (y[:, :D_MODEL] * SB_SCALE).astype(BF16)
    qkv_ref[:, D_MODEL:] = y[:, D_MODEL:].astype(BF16)


def _sb_proj_t_kernel(x_ref, g_ref, w_ref, q_t_ref, kb_ref, v_t_ref, k_ref, v_ref):
    h = _rms(x_ref[...], g_ref[...]).astype(BF16)
    y = _dot(h, w_ref[...])
    k = y[:, D_MODEL:2 * D_MODEL]
    v = y[:, 2 * D_MODEL:]
    k_ref[...] = k
    v_ref[...] = v
    kb_ref[...] = k.astype(BF16)
    q_t_ref[...] = (y[:, :D_MODEL] * (SB_SCALE * LOG2E)).T.astype(BF16)
    v_t_ref[...] = v.T.astype(BF16)


def _sb_project_t(x2d, g, w, tm):
    n = x2d.shape[0]
    row = lambda i: (i, 0)
    col = lambda i: (0, i)
    return pl.pallas_call(
        _sb_proj_t_kernel,
        grid=(n // tm,),
        in_specs=[pl.BlockSpec((tm, D_MODEL), row), _const_spec((1, D_MODEL)), _const_spec(w.shape)],
        out_specs=[pl.BlockSpec((D_MODEL, tm), col), pl.BlockSpec((tm, D_MODEL), row),
                   pl.BlockSpec((D_MODEL, tm), col), pl.BlockSpec((tm, D_MODEL), row),
                   pl.BlockSpec((tm, D_MODEL), row)],
        out_shape=[jax.ShapeDtypeStruct((D_MODEL, n), BF16), jax.ShapeDtypeStruct((n, D_MODEL), BF16),
                   jax.ShapeDtypeStruct((D_MODEL, n), BF16), jax.ShapeDtypeStruct((n, D_MODEL), F32),
                   jax.ShapeDtypeStruct((n, D_MODEL), F32)],
        compiler_params=_params(("parallel",)),
        name="sb_proj_t",
    )(x2d, g, w)


def _sb_project(x2d, g, w, tm):
    n = x2d.shape[0]
    row = lambda i: (i, 0)
    return pl.pallas_call(
        _sb_proj_kernel,
        grid=(n // tm,),
        in_specs=[pl.BlockSpec((tm, D_MODEL), row), _const_spec((1, D_MODEL)), _const_spec(w.shape)],
        out_specs=[pl.BlockSpec((tm, 3 * D_MODEL), row), pl.BlockSpec((tm, D_MODEL), row),
                   pl.BlockSpec((tm, D_MODEL), row)],
        out_shape=[jax.ShapeDtypeStruct((n, 3 * D_MODEL), BF16), jax.ShapeDtypeStruct((n, D_MODEL), F32),
                   jax.ShapeDtypeStruct((n, D_MODEL), F32)],
        compiler_params=_params(("parallel",)),
        name="sb_proj",
    )(x2d, g, w)


def _swa_proj_kernel(x_ref, g_ref, w_ref, b_ref, cos_ref, sin_ref, q_ref, k_ref, v_ref, kb_ref, vb_ref):
    h = _rms(x_ref[...], g_ref[...]).astype(BF16)
    y = _dot(h, w_ref[...]) + b_ref[...]
    cos = cos_ref[...]
    sin = sin_ref[...]
    nq = SWA_HEADS * SWA_HEAD_DIM
    nk = SWA_KV_HEADS * SWA_HEAD_DIM
    for c in range(nq // LANES):
        sl = slice(c * LANES, (c + 1) * LANES)
        q = y[:, :nq][:, sl] * cos + y[:, nq:2 * nq][:, sl] * sin
        q_ref[:, sl] = (q * SWA_SCALE).astype(BF16)
    o = 2 * nq
    for c in range(nk // LANES):
        sl = slice(c * LANES, (c + 1) * LANES)
        k = y[:, o:o + nk][:, sl] * cos + y[:, o + nk:o + 2 * nk][:, sl] * sin
        k_ref[:, sl] = k
        kb_ref[:, sl] = k.astype(BF16)
    v = y[:, o + 2 * nk:]
    v_ref[...] = v
    vb_ref[...] = v.astype(BF16)


def _swa_project(x2d, g, w, b, cos, sin, tm):
    n = x2d.shape[0]
    ntab = cos.shape[0] // tm
    nq = SWA_HEADS * SWA_HEAD_DIM
    nk = SWA_KV_HEADS * SWA_HEAD_DIM
    row = lambda i: (i, 0)
    tab = lambda i: (i % ntab, 0)
    return pl.pallas_call(
        _swa_proj_kernel,
        grid=(n // tm,),
        in_specs=[pl.BlockSpec((tm, D_MODEL), row), _const_spec((1, D_MODEL)), _const_spec(w.shape),
                  _const_spec(b.shape), pl.BlockSpec((tm, LANES), tab), pl.BlockSpec((tm, LANES), tab)],
        out_specs=[pl.BlockSpec((tm, nq), row), pl.BlockSpec((tm, nk), row), pl.BlockSpec((tm, nk), row),
                   pl.BlockSpec((tm, nk), row), pl.BlockSpec((tm, nk), row)],
        out_shape=[jax.ShapeDtypeStruct((n, nq), BF16), jax.ShapeDtypeStruct((n, nk), F32),
                   jax.ShapeDtypeStruct((n, nk), F32), jax.ShapeDtypeStruct((n, nk), BF16),
                   jax.ShapeDtypeStruct((n, nk), BF16)],
        compiler_params=_params(("parallel",)),
        name="swa_proj",
    )(x2d, g, w, b, cos, sin)


def _softmax_step(s, v, m_sc, l_sc, acc_sc):
    m_old = m_sc[...]
    m_new = jnp.maximum(m_old, jnp.max(s, axis=-1, keepdims=True))
    alpha = jnp.exp(m_old - m_new)
    p = jnp.exp(s - m_new)
    l_sc[...] = alpha * l_sc[...] + jnp.sum(p, axis=-1, keepdims=True)
    acc_sc[...] = alpha * acc_sc[...] + _dot(p.astype(BF16), v)
    m_sc[...] = m_new


def _mla_finish(wuv_ref, o_ref, l_sc, acc_sc, tq):
    o_lat = (acc_sc[...] / l_sc[...]).astype(BF16)
    for hd in range(MLA_HEADS):
        o = _dot(o_lat[hd * tq:(hd + 1) * tq], wuv_ref[hd])
        o_ref[:, hd * MLA_V:(hd + 1) * MLA_V] = o.astype(BF16)


def _mla_attn_kernel(q_ref, k_ref, v_ref, wuv_ref, o_ref, m_sc, l_sc, acc_sc, *, t):
    qi = pl.program_id(1)
    m_sc[...] = jnp.full(m_sc.shape, NEG, F32)
    l_sc[...] = jnp.zeros(l_sc.shape, F32)
    acc_sc[...] = jnp.zeros(acc_sc.shape, F32)
    k_chunk = lax.broadcasted_iota(jnp.int32, (t, 1), 0) // CHUNK
    q_chunk = lax.broadcasted_iota(jnp.int32, (1, t), 1) // CHUNK
    diag_mask = k_chunk <= q_chunk

    def step(j, mask):
        start = pl.multiple_of(j * t, t)
        k = k_ref[0, pl.ds(start, t), :]
        v = v_ref[:, pl.ds(start, t)]
        ahead = 3
        scores = [_dot(k, q_ref[hd]) for hd in range(ahead)]
        for hd in range(MLA_HEADS):
            s = scores.pop(0)
            if hd + ahead < MLA_HEADS:
                scores.append(_dot(k, q_ref[hd + ahead]))
            if mask is not None:
                s = jnp.where(mask, s, NEG)
            m_old = m_sc[hd]
            m_new = jnp.maximum(m_old, jnp.max(s, axis=0, keepdims=True))
            alpha = jnp.exp2(m_old - m_new)
            p = jnp.exp2(s - m_new)
            l_sc[hd] = alpha * l_sc[hd] + jnp.sum(p, axis=0, keepdims=True)
            acc_sc[hd] = alpha * acc_sc[hd] + _dot(v, p.astype(BF16))
            m_sc[hd] = m_new

    def body(j, _):
        step(j, None)
        return 0

    lax.fori_loop(0, qi, body, 0)
    step(qi, diag_mask)
    for hd in range(MLA_HEADS):
        o_lat = (acc_sc[hd] / l_sc[hd]).astype(BF16)
        o = _dot(wuv_ref[hd], o_lat)
        o_ref[:, hd * MLA_V:(hd + 1) * MLA_V] = o.T.astype(BF16)


def _mla_attention(q_t, kcat, ckv_t, wuv_t, b, s, t):
    n = b * s
    nq = s // t
    return pl.pallas_call(
        functools.partial(_mla_attn_kernel, t=t),
        grid=(b, nq),
        in_specs=[
            pl.BlockSpec((MLA_HEADS, MLA_QK, t), lambda bi, qi: (0, 0, bi * nq + qi)),
            pl.BlockSpec((1, s, MLA_QK), lambda bi, qi: (bi, 0, 0)),
            pl.BlockSpec((MLA_KV_LORA, s), lambda bi, qi: (0, bi)),
            _const_spec(wuv_t.shape),
        ],
        out_specs=pl.BlockSpec((t, MLA_HEADS * MLA_V), lambda bi, qi: (bi * nq + qi, 0)),
        out_shape=jax.ShapeDtypeStruct((n, MLA_HEADS * MLA_V), BF16),
        scratch_shapes=[pltpu.VMEM((MLA_HEADS, 1, t), F32), pltpu.VMEM((MLA_HEADS, 1, t), F32),
                        pltpu.VMEM((MLA_HEADS, MLA_KV_LORA, t), F32)],
        compiler_params=_params(("parallel", "arbitrary")),
        name="mla_attn",
    )(q_t, kcat.reshape(b, s, MLA_QK), ckv_t, wuv_t)


def _mla_attn_sample_kernel(q_ref, cc_ref, cr_ref, kn_ref, wuv_ref, o_ref, m_sc, l_sc, acc_sc, *, t, tk, past):
    rows = MLA_HEADS * t
    q = q_ref[...].reshape(rows, MLA_QK)
    q_lat = q[:, :MLA_KV_LORA]
    q_rope = q[:, MLA_KV_LORA:MLA_KV_LORA + MLA_ROPE]
    m_sc[...] = jnp.full(m_sc.shape, NEG, F32)
    l_sc[...] = jnp.zeros(l_sc.shape, F32)
    acc_sc[...] = jnp.zeros(acc_sc.shape, F32)

    def body(j, _):
        sl = pl.ds(pl.multiple_of(j * tk, tk), tk)
        ck = cc_ref[0, sl, :].astype(BF16)
        s = _dot_nt(q_lat, ck) + _dot_nt(q_rope, cr_ref[0, sl, :].astype(BF16))
        _softmax_step(s, ck, m_sc, l_sc, acc_sc)
        return 0

    lax.fori_loop(0, past // tk, body, 0)
    kn = kn_ref[...]
    s = _dot_nt(q, kn)
    tok = lax.broadcasted_iota(jnp.int32, (rows, 1), 0) & (t - 1)
    k_chunk = (past + lax.broadcasted_iota(jnp.int32, (1, t), 1)) // CHUNK
    s = jnp.where(k_chunk <= (past + tok) // CHUNK, s, NEG)
    _softmax_step(s, kn[:, :MLA_KV_LORA], m_sc, l_sc, acc_sc)
    _mla_finish(wuv_ref, o_ref, l_sc, acc_sc, t)


def _mla_attention_sample(q, cache_ckv, cache_kr, kcat, wuv, layer, tk):
    _, db, past, _ = cache_ckv.shape
    t = q.shape[1] // db
    rows = MLA_HEADS * t
    return pl.pallas_call(
        functools.partial(_mla_attn_sample_kernel, t=t, tk=tk, past=past),
        grid=(db,),
        in_specs=[
            pl.BlockSpec((MLA_HEADS, t, MLA_QK), lambda bi: (0, bi, 0)),
            pl.BlockSpec((None, 1, past, MLA_KV_LORA), lambda bi: (layer, bi, 0, 0)),
            pl.BlockSpec((None, 1, past, MLA_ROPE), lambda bi: (layer, bi, 0, 0)),
            pl.BlockSpec((t, MLA_QK), lambda bi: (bi, 0)),
            _const_spec(wuv.shape),
        ],
        out_specs=pl.BlockSpec((t, MLA_HEADS * MLA_V), lambda bi: (bi, 0)),
        out_shape=jax.ShapeDtypeStruct((db * t, MLA_HEADS * MLA_V), BF16),
        scratch_shapes=[pltpu.VMEM((rows, 1), F32), pltpu.VMEM((rows, 1), F32),
                        pltpu.VMEM((rows, MLA_KV_LORA), F32)],
        compiler_params=_params(("parallel",)),
        name="mla_attn_sample",
    )(q, cache_ckv, cache_kr, kcat, wuv)


def _later_key_matrix(tk):
    r = lax.broadcasted_iota(jnp.int32, (tk, tk), 0)
    c = lax.broadcasted_iota(jnp.int32, (tk, tk), 1)
    return jnp.where(r > c, 1.0, 0.0).astype(BF16)


def _sb_block(qh, k2, v2, carry, acc, u, mask):
    z = _dot_nt(qh, k2)
    log_sig = jnp.minimum(z, 0.0) - jnp.log1p(jnp.exp(-jnp.abs(z)))
    log_not = log_sig - z
    if mask is not None:
        log_not = jnp.where(mask, log_not, 0.0)
    hi = log_not.astype(BF16)
    lo = (log_not - hi.astype(F32)).astype(BF16)
    tail = _dot(hi, u) + _dot(lo, u) + carry
    a = jnp.exp(log_sig + tail)
    if mask is not None:
        a = jnp.where(mask, a, 0.0)
    acc = acc + _dot(a.astype(BF16), v2)
    carry = carry + jnp.sum(log_not, axis=-1, keepdims=True)
    return carry, acc


def _split_pair(q2):
    lane = lax.broadcasted_iota(jnp.int32, (1, LANES), 1)
    zero = jnp.zeros_like(q2)
    return jnp.where(lane < LANES // 2, q2, zero), jnp.where(lane >= LANES // 2, q2, zero), lane


def _sb_logits_stage(z, mask):
    log_sig = jnp.minimum(z, 0.0) - jnp.log2(1.0 + jnp.exp2(jnp.minimum(z, -z)))
    log_not = log_sig - z
    if mask is not None:
        log_not = jnp.where(mask, log_not, 0.0)
    hi = log_not.astype(BF16)
    lo = (log_not - hi.astype(F32)).astype(BF16)
    return log_sig, log_not, hi, lo


def _sb_attn_kernel(q_ref, k_ref, v_ref, o_ref, carry_sc, acc_sc, *, t):
    qi = pl.program_id(1)
    low = lax.broadcasted_iota(jnp.int32, (LANES, 1), 0) < LANES // 2
    r = lax.broadcasted_iota(jnp.int32, (t, t), 0)
    c = lax.broadcasted_iota(jnp.int32, (t, t), 1)
    later = jnp.where(c > r, 1.0, 0.0).astype(BF16)
    later2 = jnp.concatenate([later, later], axis=1)
    diag_mask = r < c
    carry_sc[...] = jnp.zeros(carry_sc.shape, F32)
    acc_sc[...] = jnp.zeros(acc_sc.shape, F32)

    def pick(x, hd):
        zero = jnp.zeros_like(x)
        return jnp.where(low, x, zero) if hd % 2 == 0 else jnp.where(low, zero, x)

    def block(j, mask):
        start = pl.multiple_of(j * t, t)

        def logits(hd):
            g = hd // 2
            qh = pick(q_ref[g * LANES:(g + 1) * LANES, :], hd)
            return _dot(k_ref[0, pl.ds(start, t), g * LANES:(g + 1) * LANES], qh)

        def stage1(z):
            log_sig, log_not, hi, lo = _sb_logits_stage(z, mask)
            tail = _dot(later2, jnp.concatenate([hi, lo], axis=0))
            return log_sig, tail, jnp.sum(log_not, axis=0, keepdims=True)

        def stage2(hd, log_sig, tail, colsum):
            carry = carry_sc[hd]
            a = jnp.exp2(log_sig + (tail + carry))
            if mask is not None:
                a = jnp.where(mask, a, 0.0)
            carry_sc[hd] = carry + colsum
            g = hd // 2
            vh = pick(v_ref[g * LANES:(g + 1) * LANES, pl.ds(start, t)], hd)
            acc_sc[g] += _dot(vh, a.astype(BF16))

        ahead = 2
        zs = [logits(hd) for hd in range(ahead)]
        pending = None
        for hd in range(SB_HEADS):
            st = stage1(zs.pop(0))
            if hd + ahead < SB_HEADS:
                zs.append(logits(hd + ahead))
            if pending is not None:
                stage2(hd - 1, *pending)
            pending = st
        stage2(SB_HEADS - 1, *pending)

    block(qi, diag_mask)

    def body(jj, _):
        block(qi - 1 - jj, None)
        return 0

    lax.fori_loop(0, qi, body, 0)
    for g in range(SB_HEADS // 2):
        o_ref[:, g * LANES:(g + 1) * LANES] = acc_sc[g].T.astype(BF16)


def _sb_attention(q_t, kb, v_t, b, s, t):
    n = b * s
    nq = s // t
    return pl.pallas_call(
        functools.partial(_sb_attn_kernel, t=t),
        grid=(b, nq),
        in_specs=[
            pl.BlockSpec((D_MODEL, t), lambda bi, qi: (0, bi * nq + qi)),
            pl.BlockSpec((1, s, D_MODEL), lambda bi, qi: (bi, 0, 0)),
            pl.BlockSpec((D_MODEL, s), lambda bi, qi: (0, bi)),
        ],
        out_specs=pl.BlockSpec((t, D_MODEL), lambda bi, qi: (bi * nq + qi, 0)),
        out_shape=jax.ShapeDtypeStruct((n, D_MODEL), BF16),
        scratch_shapes=[pltpu.VMEM((SB_HEADS, 1, t), F32), pltpu.VMEM((SB_HEADS // 2, LANES, t), F32)],
        compiler_params=_params(("parallel", "arbitrary")),
        name="sb_attn",
    )(q_t, kb.reshape(b, s, D_MODEL), v_t)


def _sb_attn_sample_kernel(q_ref, kn_ref, vn_ref, ck_ref, cv_ref, o_ref, *, t, tk, past):
    q_lo, q_hi, lane = _split_pair(q_ref[...])
    r = lax.broadcasted_iota(jnp.int32, (t, t), 0)
    c = lax.broadcasted_iota(jnp.int32, (t, t), 1)
    zc = jnp.zeros((t, 1), F32)
    za = jnp.zeros((t, LANES), F32)
    u_new = _later_key_matrix(t)
    kn = kn_ref[...]
    vn = vn_ref[...]
    c_lo, a_lo = _sb_block(q_lo, kn, vn, zc, za, u_new, c < r)
    c_hi, a_hi = _sb_block(q_hi, kn, vn, zc, za, u_new, c < r)
    u = _later_key_matrix(tk)

    def body(jj, state):
        j = past // tk - 1 - jj
        sl = pl.ds(pl.multiple_of(j * tk, tk), tk)
        k2 = ck_ref[0, sl, :].astype(BF16)
        v2 = cv_ref[0, sl, :].astype(BF16)
        c_lo, a_lo, c_hi, a_hi = state
        c_lo, a_lo = _sb_block(q_lo, k2, v2, c_lo, a_lo, u, None)
        c_hi, a_hi = _sb_block(q_hi, k2, v2, c_hi, a_hi, u, None)
        return c_lo, a_lo, c_hi, a_hi

    state = lax.fori_loop(0, past // tk, body, (c_lo, a_lo, c_hi, a_hi))
    o_ref[...] = jnp.where(lane < LANES // 2, state[1], state[3]).astype(BF16)


def _sb_attention_sample(qkv, cache_k, cache_v, layer, db, tk):
    n = qkv.shape[0]
    t = n // db
    past = cache_k.shape[2]
    npair = SB_HEADS // 2
    ck = cache_k.reshape(cache_k.shape[0], db, past, D_MODEL)
    cv = cache_v.reshape(cache_v.shape[0], db, past, D_MODEL)
    return pl.pallas_call(
        functools.partial(_sb_attn_sample_kernel, t=t, tk=tk, past=past),
        grid=(db, npair),
        in_specs=[
            pl.BlockSpec((t, LANES), lambda bi, g: (bi, g)),
            pl.BlockSpec((t, LANES), lambda bi, g: (bi, npair + g)),
            pl.BlockSpec((t, LANES), lambda bi, g: (bi, 2 * npair + g)),
            pl.BlockSpec((None, 1, past, LANES), lambda bi, g: (layer, bi, 0, g)),
            pl.BlockSpec((None, 1, past, LANES), lambda bi, g: (layer, bi, 0, g)),
        ],
        out_specs=pl.BlockSpec((t, LANES), lambda bi, g: (bi, g)),
        out_shape=jax.ShapeDtypeStruct((n, D_MODEL), BF16),
        compiler_params=_params(("parallel", "parallel")),
        name="sb_attn_sample",
    )(qkv, qkv, qkv, ck, cv)


def _swa_heads(sink_ref, q_ref, kband, vband, mask, o_ref):
    for blk in range(SWA_HEADS // 2):
        pair = blk // SWA_GROUP
        k2 = kband[:, pair * LANES:(pair + 1) * LANES]
        v2 = vband[:, pair * LANES:(pair + 1) * LANES]
        q_lo, q_hi, lane = _split_pair(q_ref[:, blk * LANES:(blk + 1) * LANES])
        outs = []
        for half, qh in enumerate((q_lo, q_hi)):
            s = jnp.where(mask, _dot_nt(qh, k2), NEG)
            sink = sink_ref[2 * blk + half]
            m = jnp.maximum(jnp.max(s, axis=-1, keepdims=True), sink)
            e = jnp.exp(s - m)
            den = jnp.sum(e, axis=-1, keepdims=True) + jnp.exp(sink - m)
            outs.append(_dot((e / den).astype(BF16), v2))
        o_ref[:, blk * LANES:(blk + 1) * LANES] = jnp.where(lane < LANES // 2, outs[0], outs[1]).astype(BF16)


def _swa_attn_kernel(sink_ref, q_ref, kp_ref, kc_ref, vp_ref, vc_ref, o_ref, *, tq):
    i = pl.program_id(1)
    kband = jnp.concatenate([kp_ref[0], kc_ref[0]], axis=0)
    vband = jnp.concatenate([vp_ref[0], vc_ref[0]], axis=0)
    q_pos = i * tq + lax.broadcasted_iota(jnp.int32, (tq, 1), 0)
    k_pos = i * tq - SWA_WINDOW + lax.broadcasted_iota(jnp.int32, (1, SWA_WINDOW + tq), 1)
    dc = (q_pos + SWA_WINDOW) // CHUNK - (k_pos + SWA_WINDOW) // CHUNK
    mask = (dc >= 0) & (dc <= SWA_WINDOW // CHUNK) & (k_pos >= 0)
    _swa_heads(sink_ref, q_ref, kband, vband, mask, o_ref)


def _swa_attention(sinks, q, kb, vb, b, s):
    tq = SWA_WINDOW
    nq = s // tq
    nk = SWA_KV_HEADS * SWA_HEAD_DIM
    kb3 = kb.reshape(b, s, nk)
    vb3 = vb.reshape(b, s, nk)
    prev = lambda bi, qi: (bi, jnp.maximum(qi - 1, 0), 0)
    cur = lambda bi, qi: (bi, qi, 0)
    return pl.pallas_call(
        functools.partial(_swa_attn_kernel, tq=tq),
        grid=(b, nq),
        in_specs=[
            pl.BlockSpec(memory_space=pltpu.SMEM),
            pl.BlockSpec((tq, D_MODEL), lambda bi, qi: (bi * nq + qi, 0)),
            pl.BlockSpec((1, tq, nk), prev), pl.BlockSpec((1, tq, nk), cur),
            pl.BlockSpec((1, tq, nk), prev), pl.BlockSpec((1, tq, nk), cur),
        ],
        out_specs=pl.BlockSpec((tq, D_MODEL), lambda bi, qi: (bi * nq + qi, 0)),
        out_shape=jax.ShapeDtypeStruct((b * s, D_MODEL), BF16),
        compiler_params=_params(("parallel", "arbitrary")),
        name="swa_attn",
    )(sinks, q, kb3, kb3, vb3, vb3)


def _swa_attn_sample_kernel(sink_ref, q_ref, ck_ref, cv_ref, kn_ref, vn_ref, o_ref, *, t, past):
    clen = ck_ref.shape[1]
    kband = jnp.concatenate([ck_ref[0].astype(BF16), kn_ref[...]], axis=0)
    vband = jnp.concatenate([cv_ref[0].astype(BF16), vn_ref[...]], axis=0)
    q_pos = past + lax.broadcasted_iota(jnp.int32, (t, 1), 0)
    k_pos = past - clen + lax.broadcasted_iota(jnp.int32, (1, clen + t), 1)
    dc = q_pos // CHUNK - (k_pos + SWA_WINDOW) // CHUNK + SWA_WINDOW // CHUNK
    mask = (dc >= 0) & (dc <= SWA_WINDOW // CHUNK) & (k_pos >= 0)
    _swa_heads(sink_ref, q_ref, kband, vband, mask, o_ref)


def _swa_attention_sample(sinks, q, cache_k, cache_v, kb, vb, layer, db, past):
    n = q.shape[0]
    t = n // db
    clen = cache_k.shape[2]
    nk = SWA_KV_HEADS * SWA_HEAD_DIM
    ck = cache_k.reshape(cache_k.shape[0], db, clen, nk)
    cv = cache_v.reshape(cache_v.shape[0], db, clen, nk)
    row = lambda bi: (bi, 0)
    cache = lambda bi: (layer, bi, 0, 0)
    return pl.pallas_call(
        functools.partial(_swa_attn_sample_kernel, t=t, past=past),
        grid=(db,),
        in_specs=[
            pl.BlockSpec(memory_space=pltpu.SMEM),
            pl.BlockSpec((t, D_MODEL), row),
            pl.BlockSpec((None, 1, clen, nk), cache), pl.BlockSpec((None, 1, clen, nk), cache),
            pl.BlockSpec((t, nk), row), pl.BlockSpec((t, nk), row),
        ],
        out_specs=pl.BlockSpec((t, D_MODEL), row),
        out_shape=jax.ShapeDtypeStruct((n, D_MODEL), BF16),
        compiler_params=_params(("parallel",)),
        name="swa_attn_sample",
    )(sinks, q, ck, cv, kb, vb)


def _post_kernel(x_ref, o_ref, wo_ref, gf_ref, wg_ref, wu_ref, cw_ref, cb_ref, wout_ref, st_ref,
                 p_ref, gp_ref, wpg_ref, wpp_ref, gfin_ref, y_ref, cs_ref, carry_sc, *, final):
    ti = pl.program_id(1)
    ns, tt, _ = x_ref.shape
    rows = ns * tt
    x = x_ref[...].reshape(rows, D_MODEL)
    x = x + _dot(o_ref[...].reshape(rows, o_ref.shape[-1]), wo_ref[...])
    h = _rms(x, gf_ref[...]).astype(BF16)
    t_idx = lax.broadcasted_iota(jnp.int32, (1, tt, 1), 1)

    @pl.when(ti == 0)
    def _():
        carry_sc[...] = st_ref[...]

    acc = jnp.zeros((rows, D_MODEL), F32)
    for c in range(N_FF_CHUNKS):
        sl = slice(c * FF_CHUNK, (c + 1) * FF_CHUNK)
        gate = _dot(h, wg_ref[c])
        up = _dot(h, wu_ref[c])
        st = carry_sc[:, :, sl]
        g1 = pltpu.roll(gate, 1, 0).reshape(ns, tt, FF_CHUNK)
        g2 = pltpu.roll(gate, 2, 0).reshape(ns, tt, FF_CHUNK)
        gate = gate.reshape(ns, tt, FF_CHUNK)
        g1 = jnp.where(t_idx == 0, st[:, 1:2, :], g1)
        g2 = jnp.where(t_idx == 0, st[:, 0:1, :], jnp.where(t_idx == 1, st[:, 1:2, :], g2))
        cw = cw_ref[:, sl]
        conv = g2 * cw[0:1, :] + g1 * cw[1:2, :] + gate * cw[2:3, :] + cb_ref[:, sl]
        new_st = gate[:, tt - 2:, :]
        carry_sc[:, :, sl] = new_st
        cs_ref[:, :, sl] = new_st
        act = 0.5 * conv * (1.0 + lax.erf(conv * (0.5 ** 0.5)))
        y = (act * up.reshape(ns, tt, FF_CHUNK)).reshape(rows, FF_CHUNK)
        acc = acc + _dot(y.astype(BF16), wout_ref[c])
    x = x + acc
    hp = _rms(x, gp_ref[...]).astype(BF16)
    gate_p = jax.nn.sigmoid(_dot(hp, wpg_ref[...]))
    proj = _dot(p_ref[...].reshape(rows, PLE_DIM).astype(BF16), wpp_ref[...])
    x = x + gate_p * proj
    if final:
        x = _rms(x, gfin_ref[...])
    y_ref[...] = x.reshape(ns, tt, D_MODEL)


def _post(x3, o2d, w, state, p4, layer, g_final, ns, tt, final):
    bx, tx, _ = x3.shape
    do = o2d.shape[-1]
    o3 = o2d.reshape(bx, tx, do)
    blk = lambda bi, ti: (bi, ti, 0)
    return pl.pallas_call(
        functools.partial(_post_kernel, final=final),
        grid=(bx // ns, tx // tt),
        in_specs=[
            pl.BlockSpec((ns, tt, D_MODEL), blk),
            pl.BlockSpec((ns, tt, do), blk),
            _const_spec(w["wo"].shape),
            _const_spec((1, D_MODEL)),
            _const_spec(w["wg"].shape),
            _const_spec(w["wu"].shape),
            _const_spec(w["cw"].shape),
            _const_spec(w["cb"].shape),
            _const_spec(w["wout"].shape),
            pl.BlockSpec((ns, 2, D_FF), lambda bi, ti: (bi, 0, 0)),
            pl.BlockSpec((None, ns, tt, PLE_DIM), lambda bi, ti: (layer, bi, ti, 0)),
            _const_spec((1, D_MODEL)),
            _const_spec(w["wpg"].shape),
            _const_spec(w["wpp"].shape),
            _const_spec((1, D_MODEL)),
        ],
        out_specs=[pl.BlockSpec((ns, tt, D_MODEL), blk),
                   pl.BlockSpec((ns, 2, D_FF), lambda bi, ti: (bi, 0, 0))],
        out_shape=[jax.ShapeDtypeStruct(x3.shape, F32), jax.ShapeDtypeStruct((bx, 2, D_FF), F32)],
        scratch_shapes=[pltpu.VMEM((ns, 2, D_FF), F32)],
        compiler_params=_params(("parallel", "arbitrary")),
        name="post_ffn",
    )(x3, o3, w["wo"], w["gf"], w["wg"], w["wu"], w["cw"], w["cb"], w["wout"], state, p4,
      w["gp"], w["wpg"], w["wpp"], g_final)


def _rot_cols(w):
    half = w.shape[-1] // 2
    return jnp.concatenate([-w[..., half:], w[..., :half]], axis=-1)


def _rope_tables(pos):
    half = MLA_ROPE // 2
    inv = ROPE_THETA ** (-jnp.arange(half, dtype=F32) / half)
    ang = pos.astype(F32)[:, None] * inv[None, :]
    reps = LANES // half
    return jnp.tile(jnp.cos(ang), (1, reps)), jnp.tile(jnp.sin(ang), (1, reps))


def _prep_mla(w_dq, g_q, w_uq, w_dkv, g_kv, w_uk, w_uv):
    z = jnp.zeros((D_MODEL, LANES - MLA_ROPE), F32)
    kr = w_dkv[:, MLA_KV_LORA:]
    wh = jnp.concatenate([w_dq, w_dkv[:, :MLA_KV_LORA], kr, z, _rot_cols(kr), z], axis=1)
    uq = w_uq.reshape(MLA_Q_LORA, MLA_HEADS, MLA_NOPE + MLA_ROPE)
    rp = uq[:, :, MLA_NOPE:]
    pad = jnp.zeros((MLA_Q_LORA, MLA_HEADS, LANES - MLA_ROPE), F32)
    nh = MLA_HEADS * LANES
    wuq = jnp.concatenate([uq[:, :, :MLA_NOPE].reshape(MLA_Q_LORA, nh),
                           jnp.concatenate([rp, pad], -1).reshape(MLA_Q_LORA, nh),
                           jnp.concatenate([_rot_cols(rp), pad], -1).reshape(MLA_Q_LORA, nh)], axis=1)
    wuq = wuq.astype(BF16)
    return dict(wh=wh.astype(BF16), gq=g_q[None, :], wuq=wuq, wuq_t=wuq.T, gkv=g_kv[None, :],
                wuk=jnp.transpose(w_uk, (1, 2, 0)).astype(BF16),
                wuk_t=jnp.transpose(w_uk, (1, 0, 2)).astype(BF16),
                wuv=jnp.transpose(w_uv, (1, 0, 2)).astype(BF16),
                wuv_t=jnp.transpose(w_uv, (1, 2, 0)).astype(BF16))


def _prep_swa(w_qkv, b_qkv, sinks, w_o):
    nq = SWA_HEADS * SWA_HEAD_DIM
    nk = SWA_KV_HEADS * SWA_HEAD_DIM
    perm = jnp.array(SWA_PERM)

    def split(a):
        lead = a.shape[:-1]
        q = a[..., :nq].reshape(*lead, SWA_HEADS, SWA_HEAD_DIM)[..., perm, :]
        k = a[..., nq:nq + nk].reshape(*lead, SWA_KV_HEADS, SWA_HEAD_DIM)
        return jnp.concatenate([q.reshape(*lead, nq), _rot_cols(q).reshape(*lead, nq),
                                k.reshape(*lead, nk), _rot_cols(k).reshape(*lead, nk), a[..., nq + nk:]], axis=-1)

    wo = w_o.reshape(SWA_HEADS, SWA_HEAD_DIM, D_MODEL)[perm].reshape(nq, D_MODEL)
    return split(w_qkv).astype(BF16), split(b_qkv)[None, :], sinks[perm], wo.astype(BF16)


def _prep_post(w_o, g_ffn, w_in, conv_w, conv_b, w_out, g_ple, w_gate, w_proj):
    def chunks(w):
        return jnp.transpose(w.reshape(D_MODEL, N_FF_CHUNKS, FF_CHUNK), (1, 0, 2)).astype(BF16)
    return dict(wo=w_o.astype(BF16), gf=g_ffn[None, :], wg=chunks(w_in[:, :D_FF]), wu=chunks(w_in[:, D_FF:]),
                cw=conv_w, cb=conv_b[None, :], wout=w_out.reshape(N_FF_CHUNKS, FF_CHUNK, D_MODEL).astype(BF16),
                gp=g_ple[None, :], wpg=w_gate.astype(BF16), wpp=w_proj.astype(BF16))


def kernel(x_prompt, x_sample, p_prompt, p_sample, cache_mla_ckv, cache_mla_krope, cache_sb_k, cache_sb_v, cache_swa_k, cache_swa_v, state_ffn_conv, g_mix, g_ffn, g_ple, g_final, w_mla_dq, g_mla_q, w_mla_uq, w_mla_dkv, g_mla_kv, w_mla_uk, w_mla_uv, w_mla_o, w_sb_qkv, w_sb_o, w_swa_qkv, b_swa_qkv, swa_sinks, w_swa_o, w_ffn_in, ffn_conv_w, ffn_conv_b, w_ffn_out, w_ple_gate, w_ple_proj):
    b, s, _ = x_prompt.shape
    db, t, _ = x_sample.shape
    depth = g_mix.shape[0]
    past = cache_mla_ckv.shape[2]
    n_p = b * s
    n_s = db * t
    assert s % 512 == 0 and t == CHUNK and past % 256 == 0 and n_s % 512 == 0
    assert cache_swa_k.shape[2] == SWA_WINDOW and 512 % t == 0

    tm_mla = 256
    tm = 512
    tt = 512
    ns = 512 // t
    cos_p, sin_p = _rope_tables(jnp.arange(s, dtype=jnp.int32))
    cos_s, sin_s = _rope_tables(past + (jnp.arange(tm, dtype=jnp.int32) % t))
    zero_state = jnp.zeros((b, 2, D_FF), F32)
    g_fin = g_final[None, :]

    xp, xs = x_prompt, x_sample
    outs = {k: [] for k in ("ckv_p", "kr_p", "ckv_s", "kr_s", "sbk_p", "sbv_p", "sbk_s", "sbv_s",
                            "swk_p", "swv_p", "swk_s", "swv_s", "conv_p", "conv_s")}
    for i in range(depth):
        j = i // N_MIXERS
        gm = g_mix[i][None, :]
        xp2 = xp.reshape(n_p, D_MODEL)
        xs2 = xs.reshape(n_s, D_MODEL)
        if i % N_MIXERS == 0:
            w = _prep_mla(w_mla_dq[j], g_mla_q[j], w_mla_uq[j], w_mla_dkv[j], g_mla_kv[j], w_mla_uk[j], w_mla_uv[j])
            q_p, ckv_p, kr_p, kcat_p, ckv_t = _mla_project_t(xp2, gm, w, cos_p, sin_p, tm_mla)
            q_s, ckv_s, kr_s, kcat_s = _mla_project(xs2, gm, w, cos_s[:tm_mla], sin_s[:tm_mla], tm_mla)
            o_p = _mla_attention(q_p, kcat_p, ckv_t, w["wuv_t"], b, s, 256)
            o_s = _mla_attention_sample(q_s, cache_mla_ckv, cache_mla_krope, kcat_s, w["wuv"], j, 256)
            outs["ckv_p"].append(ckv_p.reshape(b, s, MLA_KV_LORA))
            outs["kr_p"].append(kr_p.reshape(b, s, MLA_ROPE))
            outs["ckv_s"].append(ckv_s.reshape(db, t, MLA_KV_LORA))
            outs["kr_s"].append(kr_s.reshape(db, t, MLA_ROPE))
            w_o = w_mla_o[j]
        elif i % N_MIXERS == 1:
            wq = w_sb_qkv[j].astype(BF16)
            q_t, kb_p, v_t, k_p, v_p = _sb_project_t(xp2, gm, wq, tm)
            qkv_s, k_s, v_s = _sb_project(xs2, gm, wq, tm)
            o_p = _sb_attention(q_t, kb_p, v_t, b, s, 256)
            o_s = _sb_attention_sample(qkv_s, cache_sb_k, cache_sb_v, j, db, 256)
            outs["sbk_p"].append(k_p.reshape(b, s, SB_HEADS, SB_HEAD_DIM))
            outs["sbv_p"].append(v_p.reshape(b, s, SB_HEADS, SB_HEAD_DIM))
            outs["sbk_s"].append(k_s.reshape(db, t, SB_HEADS, SB_HEAD_DIM))
            outs["sbv_s"].append(v_s.reshape(db, t, SB_HEADS, SB_HEAD_DIM))
            w_o = w_sb_o[j]
        else:
            wq, bq, sinks, w_o = _prep_swa(w_swa_qkv[j], b_swa_qkv[j], swa_sinks[j], w_swa_o[j])
            q_p, k_p, v_p, kb_p, vb_p = _swa_project(xp2, gm, wq, bq, cos_p, sin_p, tm)
            q_s, k_s, v_s, kb_s, vb_s = _swa_project(xs2, gm, wq, bq, cos_s, sin_s, tm)
            o_p = _swa_attention(sinks, q_p, kb_p, vb_p, b, s)
            o_s = _swa_attention_sample(sinks, q_s, cache_swa_k, cache_swa_v, kb_s, vb_s, j, db, past)
            kv_shape = (SWA_KV_HEADS, SWA_HEAD_DIM)
            outs["swk_p"].append(k_p.reshape(b, s, *kv_shape)[:, -SWA_WINDOW:])
            outs["swv_p"].append(v_p.reshape(b, s, *kv_shape)[:, -SWA_WINDOW:])
            outs["swk_s"].append(jnp.concatenate([cache_swa_k[j], k_s.reshape(db, t, *kv_shape)], 1)[:, -SWA_WINDOW:])
            outs["swv_s"].append(jnp.concatenate([cache_swa_v[j], v_s.reshape(db, t, *kv_shape)], 1)[:, -SWA_WINDOW:])
        wp = _prep_post(w_o, g_ffn[i], w_ffn_in[i], ffn_conv_w[i], ffn_conv_b[i], w_ffn_out[i],
                        g_ple[i], w_ple_gate[i], w_ple_proj[i])
        final = i == depth - 1
        xp, cp = _post(xp, o_p, wp, zero_state, p_prompt, i, g_fin, 1, tt, final)
        xs, cs = _post(xs, o_s, wp, state_ffn_conv[i], p_sample, i, g_fin, ns, t, final)
        outs["conv_p"].append(cp)
        outs["conv_s"].append(cs)
    order = ("ckv_p", "kr_p", "ckv_s", "kr_s", "sbk_p", "sbv_p", "sbk_s", "sbv_s",
             "swk_p", "swv_p", "swk_s", "swv_s", "conv_p", "conv_s")
    return (xp, xs) + tuple(jnp.stack(outs[k]) for k in order)
```

```python
import functools

import jax
import jax.numpy as jnp
from jax import lax
from jax.experimental import pallas as pl
from jax.experimental.pallas import tpu as pltpu

F32 = jnp.float32
BF16 = jnp.bfloat16

D_MODEL = 1024
CHUNK = 64
N_MIXERS = 3
RMS_EPS = 1e-6
ROPE_THETA = 10000.0
NEG = -1e30
LOG2E = 1.4426950408889634

MLA_HEADS = 16
MLA_NOPE = 128
MLA_ROPE = 64
MLA_V = 128
MLA_Q_LORA = 384
MLA_KV_LORA = 256
MLA_SCALE = (MLA_NOPE + MLA_ROPE) ** -0.5
MLA_QK = MLA_KV_LORA + 128
MLA_KDIM = MLA_NOPE + 128
MLA_HEAD_GROUP = 8

SB_HEADS = 16
SB_HEAD_DIM = 64
SB_SCALE = SB_HEAD_DIM ** -0.5

SWA_HEADS = 16
SWA_KV_HEADS = 4
SWA_GROUP = SWA_HEADS // SWA_KV_HEADS
SWA_HEAD_DIM = 64
SWA_WINDOW = 128
SWA_SCALE = SWA_HEAD_DIM ** -0.5
SWA_PERM = tuple(8 * p + 4 * s + g for p in range(2) for g in range(SWA_GROUP) for s in range(2))

D_FF = 2816
FF_CHUNK = 256
N_FF_CHUNKS = D_FF // FF_CHUNK
PLE_DIM = 256

LANES = 128
VMEM_LIMIT = 56 * 1024 * 1024


def _dot(a, b):
    return jnp.dot(a, b, preferred_element_type=F32)


def _dot_nt(a, b):
    return lax.dot_general(a, b, (((1,), (1,)), ((), ())), preferred_element_type=F32)


def _rms(x, g):
    ms = jnp.mean(x * x, axis=-1, keepdims=True)
    return x * lax.rsqrt(ms + RMS_EPS) * g


def _const_spec(shape):
    nd = len(shape)
    return pl.BlockSpec(shape, lambda *_: (0,) * nd, pipeline_mode=pl.Buffered(1))


def _params(sem):
    return pltpu.CompilerParams(dimension_semantics=sem, vmem_limit_bytes=VMEM_LIMIT)


def _mla_latents(x_ref, g_ref, wh_ref, gq_ref, gkv_ref, cos, sin, ckv_ref, kr_ref, kcat_ref):
    h = _rms(x_ref[...], g_ref[...]).astype(BF16)
    y = _dot(h, wh_ref[...])
    cq = _rms(y[:, :MLA_Q_LORA], gq_ref[...])
    o = MLA_Q_LORA
    ckv = _rms(y[:, o:o + MLA_KV_LORA], gkv_ref[...])
    o += MLA_KV_LORA
    kr = y[:, o:o + LANES] * cos + y[:, o + LANES:o + 2 * LANES] * sin
    ckv_ref[...] = ckv
    kr_ref[...] = kr[:, :MLA_ROPE]
    if kcat_ref is not None:
        kcat_ref[...] = jnp.concatenate([ckv, kr], axis=-1).astype(BF16)
    return cq, ckv, kr


def _mla_proj_t_kernel(x_ref, g_ref, wh_ref, gq_ref, wuq_ref, wuk_ref, wuv_ref, gkv_ref, cos_ref, sin_ref,
                       cos_t_ref, sin_t_ref, q_ref, ckv_ref, kr_ref, k_ref, v_ref):
    cq, ckv, kr = _mla_latents(x_ref, g_ref, wh_ref, gq_ref, gkv_ref, cos_ref[...], sin_ref[...],
                               ckv_ref, kr_ref, None)
    ckv_b = ckv.astype(BF16)
    kn = _dot(ckv_b, wuk_ref[...])
    kr_b = kr.astype(BF16)
    for hd in range(MLA_HEADS):
        k_ref[:, hd * MLA_KDIM:hd * MLA_KDIM + LANES] = kn[:, hd * LANES:(hd + 1) * LANES].astype(BF16)
        k_ref[:, hd * MLA_KDIM + LANES:(hd + 1) * MLA_KDIM] = kr_b
    v_ref[...] = _dot(wuv_ref[...], ckv.T.astype(BF16)).astype(BF16)
    qa = _dot(wuq_ref[...], cq.T.astype(BF16))
    cos = cos_t_ref[...]
    sin = sin_t_ref[...]
    nh = MLA_HEADS * LANES
    for hd in range(MLA_HEADS):
        sl = slice(hd * LANES, (hd + 1) * LANES)
        qr = qa[nh:2 * nh][sl] * cos + qa[2 * nh:][sl] * sin
        q_ref[hd] = (jnp.concatenate([qa[sl], qr], axis=0) * (MLA_SCALE * LOG2E)).astype(BF16)


def _mla_project_t(x2d, g, w, cos, sin, tm):
    n = x2d.shape[0]
    ntab = cos.shape[0] // tm
    row = lambda i: (i, 0)
    col = lambda i: (0, i)
    return pl.pallas_call(
        _mla_proj_t_kernel,
        grid=(n // tm,),
        in_specs=[
            pl.BlockSpec((tm, D_MODEL), row),
            _const_spec((1, D_MODEL)),
            _const_spec(w["wh"].shape),
            _const_spec((1, MLA_Q_LORA)),
            _const_spec(w["wuq_t"].shape),
            _const_spec(w["wuk_flat"].shape),
            _const_spec(w["wuv_flat_t"].shape),
            _const_spec((1, MLA_KV_LORA)),
            pl.BlockSpec((tm, LANES), lambda i: (i % ntab, 0)),
            pl.BlockSpec((tm, LANES), lambda i: (i % ntab, 0)),
            pl.BlockSpec((LANES, tm), lambda i: (0, i % ntab)),
            pl.BlockSpec((LANES, tm), lambda i: (0, i % ntab)),
        ],
        out_specs=[
            pl.BlockSpec((MLA_HEADS, MLA_KDIM, tm), lambda i: (0, 0, i)),
            pl.BlockSpec((tm, MLA_KV_LORA), row),
            pl.BlockSpec((tm, MLA_ROPE), row),
            pl.BlockSpec((tm, MLA_HEADS * MLA_KDIM), row),
            pl.BlockSpec((MLA_HEADS * MLA_V, tm), col),
        ],
        out_shape=[
            jax.ShapeDtypeStruct((MLA_HEADS, MLA_KDIM, n), BF16),
            jax.ShapeDtypeStruct((n, MLA_KV_LORA), F32),
            jax.ShapeDtypeStruct((n, MLA_ROPE), F32),
            jax.ShapeDtypeStruct((n, MLA_HEADS * MLA_KDIM), BF16),
            jax.ShapeDtypeStruct((MLA_HEADS * MLA_V, n), BF16),
        ],
        compiler_params=_params(("parallel",)),
        name="mla_proj_t",
    )(x2d, g, w["wh"], w["gq"], w["wuq_t"], w["wuk_flat"], w["wuv_flat_t"], w["gkv"], cos, sin, cos.T, sin.T)


def _mla_proj_kernel(x_ref, g_ref, wh_ref, gq_ref, wuq_ref, wuk_ref, gkv_ref, cos_ref, sin_ref,
                     q_ref, ckv_ref, kr_ref, kcat_ref):
    cos = cos_ref[...]
    sin = sin_ref[...]
    cq, _, _ = _mla_latents(x_ref, g_ref, wh_ref, gq_ref, gkv_ref, cos, sin, ckv_ref, kr_ref, kcat_ref)
    qa = _dot(cq.astype(BF16), wuq_ref[...])
    nh = MLA_HEADS * LANES
    for hd in range(MLA_HEADS):
        sl = slice(hd * LANES, (hd + 1) * LANES)
        ql = _dot(qa[:, sl].astype(BF16), wuk_ref[hd])
        qr = qa[:, nh:2 * nh][:, sl] * cos + qa[:, 2 * nh:][:, sl] * sin
        q_ref[hd] = (jnp.concatenate([ql, qr], axis=-1) * MLA_SCALE).astype(BF16)


def _mla_project(x2d, g, w, cos, sin, tm):
    n = x2d.shape[0]
    ntab = cos.shape[0] // tm
    row = lambda i: (i, 0)
    tab = lambda i: (i % ntab, 0)
    return pl.pallas_call(
        _mla_proj_kernel,
        grid=(n // tm,),
        in_specs=[
            pl.BlockSpec((tm, D_MODEL), row),
            _const_spec((1, D_MODEL)),
            _const_spec(w["wh"].shape),
            _const_spec((1, MLA_Q_LORA)),
            _const_spec(w["wuq"].shape),
            _const_spec(w["wuk"].shape),
            _const_spec((1, MLA_KV_LORA)),
            pl.BlockSpec((tm, LANES), tab),
            pl.BlockSpec((tm, LANES), tab),
        ],
        out_specs=[
            pl.BlockSpec((MLA_HEADS, tm, MLA_QK), lambda i: (0, i, 0)),
            pl.BlockSpec((tm, MLA_KV_LORA), row),
            pl.BlockSpec((tm, MLA_ROPE), row),
            pl.BlockSpec((tm, MLA_QK), row),
        ],
        out_shape=[
            jax.ShapeDtypeStruct((MLA_HEADS, n, MLA_QK), BF16),
            jax.ShapeDtypeStruct((n, MLA_KV_LORA), F32),
            jax.ShapeDtypeStruct((n, MLA_ROPE), F32),
            jax.ShapeDtypeStruct((n, MLA_QK), BF16),
        ],
        compiler_params=_params(("parallel",)),
        name="mla_proj",
    )(x2d, g, w["wh"], w["gq"], w["wuq"], w["wuk"], w["gkv"], cos, sin)


def _sb_proj_kernel(x_ref, g_ref, w_ref, qkv_ref, k_ref, v_ref):
    h = _rms(x_ref[...], g_ref[...]).astype(BF16)
    y = _dot(h, w_ref[...])
    k_ref[...] = y[:, D_MODEL:2 * D_MODEL]
    v_ref[...] = y[:, 2 * D_MODEL:]
    qkv_ref[:, :D_MODEL] = (y[:, :D_MODEL] * (SB_SCALE * LOG2E)).astype(BF16)
    qkv_ref[:, D_MODEL:] = y[:, D_MODEL:].astype(BF16)


def _sb_proj_t_kernel(x_ref, g_ref, w_ref, q_t_ref, kb_ref, v_t_ref, k_ref, v_ref):
    h = _rms(x_ref[...], g_ref[...]).astype(BF16)
    y = _dot(h, w_ref[...])
    k = y[:, D_MODEL:2 * D_MODEL]
    v = y[:, 2 * D_MODEL:]
    k_ref[...] = k
    v_ref[...] = v
    kb_ref[...] = k.astype(BF16)
    q_t_ref[...] = (y[:, :D_MODEL] * (SB_SCALE * LOG2E)).T.astype(BF16)
    v_t_ref[...] = v.T.astype(BF16)


def _sb_project_t(x2d, g, w, tm):
    n = x2d.shape[0]
    row = lambda i: (i, 0)
    col = lambda i: (0, i)
    return pl.pallas_call(
        _sb_proj_t_kernel,
        grid=(n // tm,),
        in_specs=[pl.BlockSpec((tm, D_MODEL), row), _const_spec((1, D_MODEL)), _const_spec(w.shape)],
        out_specs=[pl.BlockSpec((D_MODEL, tm), col), pl.BlockSpec((tm, D_MODEL), row),
                   pl.BlockSpec((D_MODEL, tm), col), pl.BlockSpec((tm, D_MODEL), row),
                   pl.BlockSpec((tm, D_MODEL), row)],
        out_shape=[jax.ShapeDtypeStruct((D_MODEL, n), BF16), jax.ShapeDtypeStruct((n, D_MODEL), BF16),
                   jax.ShapeDtypeStruct((D_MODEL, n), BF16), jax.ShapeDtypeStruct((n, D_MODEL), F32),
                   jax.ShapeDtypeStruct((n, D_MODEL), F32)],
        compiler_params=_params(("parallel",)),
        name="sb_proj_t",
    )(x2d, g, w)


def _sb_project(x2d, g, w, tm):
    n = x2d.shape[0]
    row = lambda i: (i, 0)
    return pl.pallas_call(
        _sb_proj_kernel,
        grid=(n // tm,),
        in_specs=[pl.BlockSpec((tm, D_MODEL), row), _const_spec((1, D_MODEL)), _const_spec(w.shape)],
        out_specs=[pl.BlockSpec((tm, 3 * D_MODEL), row), pl.BlockSpec((tm, D_MODEL), row),
                   pl.BlockSpec((tm, D_MODEL), row)],
        out_shape=[jax.ShapeDtypeStruct((n, 3 * D_MODEL), BF16), jax.ShapeDtypeStruct((n, D_MODEL), F32),
                   jax.ShapeDtypeStruct((n, D_MODEL), F32)],
        compiler_params=_params(("parallel",)),
        name="sb_proj",
    )(x2d, g, w)


def _swa_proj_kernel(x_ref, g_ref, w_ref, b_ref, cos_ref, sin_ref, q_ref, k_ref, v_ref, kb_ref, vb_ref):
    h = _rms(x_ref[...], g_ref[...]).astype(BF16)
    y = _dot(h, w_ref[...]) + b_ref[...]
    cos = cos_ref[...]
    sin = sin_ref[...]
    nq = SWA_HEADS * SWA_HEAD_DIM
    nk = SWA_KV_HEADS * SWA_HEAD_DIM
    for c in range(nq // LANES):
        sl = slice(c * LANES, (c + 1) * LANES)
        q = y[:, :nq][:, sl] * cos + y[:, nq:2 * nq][:, sl] * sin
        q_ref[:, sl] = (q * SWA_SCALE).astype(BF16)
    o = 2 * nq
    for c in range(nk // LANES):
        sl = slice(c * LANES, (c + 1) * LANES)
        k = y[:, o:o + nk][:, sl] * cos + y[:, o + nk:o + 2 * nk][:, sl] * sin
        k_ref[:, sl] = k
        kb_ref[:, sl] = k.astype(BF16)
    v = y[:, o + 2 * nk:]
    v_ref[...] = v
    vb_ref[...] = v.astype(BF16)


def _swa_project(x2d, g, w, b, cos, sin, tm):
    n = x2d.shape[0]
    ntab = cos.shape[0] // tm
    nq = SWA_HEADS * SWA_HEAD_DIM
    nk = SWA_KV_HEADS * SWA_HEAD_DIM
    row = lambda i: (i, 0)
    tab = lambda i: (i % ntab, 0)
    return pl.pallas_call(
        _swa_proj_kernel,
        grid=(n // tm,),
        in_specs=[pl.BlockSpec((tm, D_MODEL), row), _const_spec((1, D_MODEL)), _const_spec(w.shape),
                  _const_spec(b.shape), pl.BlockSpec((tm, LANES), tab), pl.BlockSpec((tm, LANES), tab)],
        out_specs=[pl.BlockSpec((tm, nq), row), pl.BlockSpec((tm, nk), row), pl.BlockSpec((tm, nk), row),
                   pl.BlockSpec((tm, nk), row), pl.BlockSpec((tm, nk), row)],
        out_shape=[jax.ShapeDtypeStruct((n, nq), BF16), jax.ShapeDtypeStruct((n, nk), F32),
                   jax.ShapeDtypeStruct((n, nk), F32), jax.ShapeDtypeStruct((n, nk), BF16),
                   jax.ShapeDtypeStruct((n, nk), BF16)],
        compiler_params=_params(("parallel",)),
        name="swa_proj",
    )(x2d, g, w, b, cos, sin)


def _mla_attn_kernel(q_ref, k_ref, v_ref, o_ref, m_sc, l_sc, acc_sc, *, t):
    qi = pl.program_id(2)
    nh = q_ref.shape[0]
    m_sc[...] = jnp.full(m_sc.shape, NEG, F32)
    l_sc[...] = jnp.zeros(l_sc.shape, F32)
    acc_sc[...] = jnp.zeros(acc_sc.shape, F32)
    k_chunk = lax.broadcasted_iota(jnp.int32, (t, 1), 0) // CHUNK
    q_chunk = lax.broadcasted_iota(jnp.int32, (1, t), 1) // CHUNK
    diag_mask = k_chunk <= q_chunk

    def step(j, mask):
        start = pl.multiple_of(j * t, t)

        def scores(hd):
            return _dot(k_ref[0, pl.ds(start, t), hd * MLA_KDIM:(hd + 1) * MLA_KDIM], q_ref[hd])

        ahead = 3
        pending = [scores(hd) for hd in range(ahead)]
        for hd in range(nh):
            s = pending.pop(0)
            if hd + ahead < nh:
                pending.append(scores(hd + ahead))
            if mask is not None:
                s = jnp.where(mask, s, NEG)
            m_old = m_sc[hd]
            m_new = jnp.maximum(m_old, jnp.max(s, axis=0, keepdims=True))
            alpha = jnp.exp2(m_old - m_new)
            p = jnp.exp2(s - m_new)
            l_sc[hd] = alpha * l_sc[hd] + jnp.sum(p, axis=0, keepdims=True)
            v = v_ref[hd * MLA_V:(hd + 1) * MLA_V, pl.ds(start, t)]
            acc_sc[hd] = alpha * acc_sc[hd] + _dot(v, p.astype(BF16))
            m_sc[hd] = m_new

    def body(j, _):
        step(j, None)
        return 0

    lax.fori_loop(0, qi, body, 0)
    step(qi, diag_mask)
    for hd in range(nh):
        o_ref[:, hd * MLA_V:(hd + 1) * MLA_V] = (acc_sc[hd] / l_sc[hd]).T.astype(BF16)


def _mla_attention(q_t, k_full, v_t, b, s, t):
    n = b * s
    nq = s // t
    g = MLA_HEAD_GROUP
    return pl.pallas_call(
        functools.partial(_mla_attn_kernel, t=t),
        grid=(b, MLA_HEADS // g, nq),
        in_specs=[
            pl.BlockSpec((g, MLA_KDIM, t), lambda bi, hg, qi: (hg, 0, bi * nq + qi)),
            pl.BlockSpec((1, s, g * MLA_KDIM), lambda bi, hg, qi: (bi, 0, hg)),
            pl.BlockSpec((g * MLA_V, s), lambda bi, hg, qi: (hg, bi)),
        ],
        out_specs=pl.BlockSpec((t, g * MLA_V), lambda bi, hg, qi: (bi * nq + qi, hg)),
        out_shape=jax.ShapeDtypeStruct((n, MLA_HEADS * MLA_V), BF16),
        scratch_shapes=[pltpu.VMEM((g, 1, t), F32), pltpu.VMEM((g, 1, t), F32),
                        pltpu.VMEM((g, MLA_V, t), F32)],
        compiler_params=_params(("parallel", "parallel", "arbitrary")),
        name="mla_attn",
    )(q_t, k_full.reshape(b, s, MLA_HEADS * MLA_KDIM), v_t)


def _mla_attn_sample_kernel(q_ref, cc_ref, cr_ref, kn_ref, wuv_ref, o_ref, *, t, past):
    rows = MLA_HEADS * t
    q = q_ref[...].reshape(rows, MLA_QK)
    ck = cc_ref[0].astype(BF16)
    s_old = (_dot_nt(q[:, :MLA_KV_LORA], ck)
             + _dot_nt(q[:, MLA_KV_LORA:MLA_KV_LORA + MLA_ROPE], cr_ref[0].astype(BF16)))
    kn = kn_ref[...]
    s_new = _dot_nt(q, kn)
    tok = lax.broadcasted_iota(jnp.int32, (rows, 1), 0) & (t - 1)
    k_chunk = (past + lax.broadcasted_iota(jnp.int32, (1, t), 1)) // CHUNK
    s_new = jnp.where(k_chunk <= (past + tok) // CHUNK, s_new, NEG)
    m = jnp.maximum(jnp.max(s_old, axis=-1, keepdims=True), jnp.max(s_new, axis=-1, keepdims=True))
    p_old = jnp.exp(s_old - m)
    p_new = jnp.exp(s_new - m)
    l = jnp.sum(p_old, axis=-1, keepdims=True) + jnp.sum(p_new, axis=-1, keepdims=True)
    acc = _dot(p_old.astype(BF16), ck) + _dot(p_new.astype(BF16), kn[:, :MLA_KV_LORA])
    o_lat = (acc / l).astype(BF16)
    for hd in range(MLA_HEADS):
        o = _dot(o_lat[hd * t:(hd + 1) * t], wuv_ref[hd])
        o_ref[:, hd * MLA_V:(hd + 1) * MLA_V] = o.astype(BF16)


def _mla_attention_sample(q, cache_ckv, cache_kr, kcat, wuv, layer):
    _, db, past, _ = cache_ckv.shape
    t = q.shape[1] // db
    return pl.pallas_call(
        functools.partial(_mla_attn_sample_kernel, t=t, past=past),
        grid=(db,),
        in_specs=[
            pl.BlockSpec((MLA_HEADS, t, MLA_QK), lambda bi: (0, bi, 0)),
            pl.BlockSpec((None, 1, past, MLA_KV_LORA), lambda bi: (layer, bi, 0, 0)),
            pl.BlockSpec((None, 1, past, MLA_ROPE), lambda bi: (layer, bi, 0, 0)),
            pl.BlockSpec((t, MLA_QK), lambda bi: (bi, 0)),
            _const_spec(wuv.shape),
        ],
        out_specs=pl.BlockSpec((t, MLA_HEADS * MLA_V), lambda bi: (bi, 0)),
        out_shape=jax.ShapeDtypeStruct((db * t, MLA_HEADS * MLA_V), BF16),
        compiler_params=_params(("parallel",)),
        name="mla_attn_sample",
    )(q, cache_ckv, cache_kr, kcat, wuv)


def _later_key_matrix(tk):
    r = lax.broadcasted_iota(jnp.int32, (tk, tk), 0)
    c = lax.broadcasted_iota(jnp.int32, (tk, tk), 1)
    return jnp.where(r > c, 1.0, 0.0).astype(BF16)


def _split_pair(q2):
    lane = lax.broadcasted_iota(jnp.int32, (1, LANES), 1)
    zero = jnp.zeros_like(q2)
    return jnp.where(lane < LANES // 2, q2, zero), jnp.where(lane >= LANES // 2, q2, zero), lane


def _sb_logits_stage(z, mask):
    log_sig = jnp.minimum(z, 0.0) - jnp.log2(1.0 + jnp.exp2(jnp.minimum(z, -z)))
    log_not = log_sig - z
    if mask is not None:
        log_not = jnp.where(mask, log_not, 0.0)
    hi = log_not.astype(BF16)
    lo = (log_not - hi.astype(F32)).astype(BF16)
    return log_sig, log_not, hi, lo


def _sb_attn_kernel(q_ref, k_ref, v_ref, o_ref, carry_sc, acc_sc, *, t):
    qi = pl.program_id(1)
    low = lax.broadcasted_iota(jnp.int32, (LANES, 1), 0) < LANES // 2
    r = lax.broadcasted_iota(jnp.int32, (t, t), 0)
    c = lax.broadcasted_iota(jnp.int32, (t, t), 1)
    later = jnp.where(c > r, 1.0, 0.0).astype(BF16)
    later2 = jnp.concatenate([later, later], axis=1)
    diag_mask = r < c
    carry_sc[...] = jnp.zeros(carry_sc.shape, F32)
    acc_sc[...] = jnp.zeros(acc_sc.shape, F32)

    def pick(x, hd):
        zero = jnp.zeros_like(x)
        return jnp.where(low, x, zero) if hd % 2 == 0 else jnp.where(low, zero, x)

    def block(j, mask):
        start = pl.multiple_of(j * t, t)

        def logits(hd):
            g = hd // 2
            qh = pick(q_ref[g * LANES:(g + 1) * LANES, :], hd)
            return _dot(k_ref[0, pl.ds(start, t), g * LANES:(g + 1) * LANES], qh)

        def stage1(z):
            log_sig, log_not, hi, lo = _sb_logits_stage(z, mask)
            tail = _dot(later2, jnp.concatenate([hi, lo], axis=0))
            return log_sig, tail, jnp.sum(log_not, axis=0, keepdims=True)

        def stage2(hd, log_sig, tail, colsum):
            carry = carry_sc[hd]
            a = jnp.exp2(log_sig + (tail + carry))
            if mask is not None:
                a = jnp.where(mask, a, 0.0)
            carry_sc[hd] = carry + colsum
            g = hd // 2
            vh = pick(v_ref[g * LANES:(g + 1) * LANES, pl.ds(start, t)], hd)
            acc_sc[g] += _dot(vh, a.astype(BF16))

        ahead = 2
        zs = [logits(hd) for hd in range(ahead)]
        pending = None
        for hd in range(SB_HEADS):
            st = stage1(zs.pop(0))
            if hd + ahead < SB_HEADS:
                zs.append(logits(hd + ahead))
            if pending is not None:
                stage2(hd - 1, *pending)
            pending = st
        stage2(SB_HEADS - 1, *pending)

    block(qi, diag_mask)

    def body(jj, _):
        block(qi - 1 - jj, None)
        return 0

    lax.fori_loop(0, qi, body, 0)
    for g in range(SB_HEADS // 2):
        o_ref[:, g * LANES:(g + 1) * LANES] = acc_sc[g].T.astype(BF16)


def _sb_attention(q_t, kb, v_t, b, s, t):
    n = b * s
    nq = s // t
    return pl.pallas_call(
        functools.partial(_sb_attn_kernel, t=t),
        grid=(b, nq),
        in_specs=[
            pl.BlockSpec((D_MODEL, t), lambda bi, qi: (0, bi * nq + qi)),
            pl.BlockSpec((1, s, D_MODEL), lambda bi, qi: (bi, 0, 0)),
            pl.BlockSpec((D_MODEL, s), lambda bi, qi: (0, bi)),
        ],
        out_specs=pl.BlockSpec((t, D_MODEL), lambda bi, qi: (bi * nq + qi, 0)),
        out_shape=jax.ShapeDtypeStruct((n, D_MODEL), BF16),
        scratch_shapes=[pltpu.VMEM((SB_HEADS, 1, t), F32), pltpu.VMEM((SB_HEADS // 2, LANES, t), F32)],
        compiler_params=_params(("parallel", "arbitrary")),
        name="sb_attn",
    )(q_t, kb.reshape(b, s, D_MODEL), v_t)


def _sb_attn_sample_kernel(q_ref, kn_ref, vn_ref, ck_ref, cv_ref, o_ref, carry_sc, acc_sc, *, t, tk):
    j = pl.program_id(1)
    hdim = SB_HEAD_DIM

    def head(ref, hd):
        return ref[:, hd * hdim:(hd + 1) * hdim]

    def run(keys, values, later2, mask, first):
        def logits(hd):
            return _dot_nt(head(q_ref, hd), keys(hd))

        def stage1(z):
            log_sig, log_not, hi, lo = _sb_logits_stage(z, mask)
            tail = _dot(jnp.concatenate([hi, lo], axis=1), later2)
            return log_sig, tail, jnp.sum(log_not, axis=-1, keepdims=True)

        def stage2(hd, log_sig, tail, rowsum):
            if first:
                a = jnp.exp2(log_sig + tail)
                carry_sc[hd] = rowsum
            else:
                carry = carry_sc[hd]
                a = jnp.exp2(log_sig + (tail + carry))
                carry_sc[hd] = carry + rowsum
            if mask is not None:
                a = jnp.where(mask, a, 0.0)
            o = _dot(a.astype(BF16), values(hd))
            acc_sc[hd] = o if first else acc_sc[hd] + o

        ahead, lag = 4, 2
        zs = [logits(hd) for hd in range(ahead)]
        pending = []
        for hd in range(SB_HEADS):
            pending.append((hd,) + stage1(zs.pop(0)))
            if hd + ahead < SB_HEADS:
                zs.append(logits(hd + ahead))
            if len(pending) > lag:
                stage2(*pending.pop(0))
        for item in pending:
            stage2(*item)

    def later_pair(n):
        u = _later_key_matrix(n)
        return jnp.concatenate([u, u], axis=0)

    @pl.when(j == 0)
    def _():
        r = lax.broadcasted_iota(jnp.int32, (t, t), 0)
        c = lax.broadcasted_iota(jnp.int32, (t, t), 1)
        run(lambda hd: head(kn_ref, hd), lambda hd: head(vn_ref, hd), later_pair(t), c < r, True)

    def cached(ref, hd):
        return ref[0, pl.ds(hd, tk, stride=SB_HEADS), :].astype(BF16)

    run(lambda hd: cached(ck_ref, hd), lambda hd: cached(cv_ref, hd), later_pair(tk), None, False)

    @pl.when(j == pl.num_programs(1) - 1)
    def _():
        o_ref[...] = jnp.concatenate([acc_sc[hd] for hd in range(SB_HEADS)], axis=-1).astype(BF16)


def _sb_attention_sample(qkv, cache_k, cache_v, layer, db, tk):
    n = qkv.shape[0]
    t = n // db
    past = cache_k.shape[2]
    nblk = past // tk
    cache = lambda bi, j: (layer, bi, nblk - 1 - j, 0)
    cache_k = cache_k.reshape(cache_k.shape[0], db, past * SB_HEADS, SB_HEAD_DIM)
    cache_v = cache_v.reshape(cache_v.shape[0], db, past * SB_HEADS, SB_HEAD_DIM)
    return pl.pallas_call(
        functools.partial(_sb_attn_sample_kernel, t=t, tk=tk),
        grid=(db, nblk),
        in_specs=[
            pl.BlockSpec((t, D_MODEL), lambda bi, j: (bi, 0)),
            pl.BlockSpec((t, D_MODEL), lambda bi, j: (bi, 1)),
            pl.BlockSpec((t, D_MODEL), lambda bi, j: (bi, 2)),
            pl.BlockSpec((None, 1, tk * SB_HEADS, SB_HEAD_DIM), cache),
            pl.BlockSpec((None, 1, tk * SB_HEADS, SB_HEAD_DIM), cache),
        ],
        out_specs=pl.BlockSpec((t, D_MODEL), lambda bi, j: (bi, 0)),
        out_shape=jax.ShapeDtypeStruct((n, D_MODEL), BF16),
        scratch_shapes=[pltpu.VMEM((SB_HEADS, t, 1), F32), pltpu.VMEM((SB_HEADS, t, SB_HEAD_DIM), F32)],
        compiler_params=_params(("parallel", "arbitrary")),
        name="sb_attn_sample",
    )(qkv, qkv, qkv, cache_k, cache_v)


def _swa_heads(sink_ref, q_ref, kband, vband, mask, o_ref):
    for blk in range(SWA_HEADS // 2):
        pair = blk // SWA_GROUP
        k2 = kband[:, pair * LANES:(pair + 1) * LANES]
        v2 = vband[:, pair * LANES:(pair + 1) * LANES]
        q_lo, q_hi, lane = _split_pair(q_ref[:, blk * LANES:(blk + 1) * LANES])
        outs = []
        for half, qh in enumerate((q_lo, q_hi)):
            s = jnp.where(mask, _dot_nt(qh, k2), NEG)
            sink = sink_ref[2 * blk + half]
            m = jnp.maximum(jnp.max(s, axis=-1, keepdims=True), sink)
            e = jnp.exp(s - m)
            den = jnp.sum(e, axis=-1, keepdims=True) + jnp.exp(sink - m)
            outs.append(_dot((e / den).astype(BF16), v2))
        o_ref[:, blk * LANES:(blk + 1) * LANES] = jnp.where(lane < LANES // 2, outs[0], outs[1]).astype(BF16)


def _swa_attn_kernel(sink_ref, q_ref, kp_ref, kc_ref, vp_ref, vc_ref, o_ref, *, tq):
    i = pl.program_id(1)
    kband = jnp.concatenate([kp_ref[0], kc_ref[0]], axis=0)
    vband = jnp.concatenate([vp_ref[0], vc_ref[0]], axis=0)
    q_pos = i * tq + lax.broadcasted_iota(jnp.int32, (tq, 1), 0)
    k_pos = i * tq - SWA_WINDOW + lax.broadcasted_iota(jnp.int32, (1, SWA_WINDOW + tq), 1)
    dc = (q_pos + SWA_WINDOW) // CHUNK - (k_pos + SWA_WINDOW) // CHUNK
    mask = (dc >= 0) & (dc <= SWA_WINDOW // CHUNK) & (k_pos >= 0)
    _swa_heads(sink_ref, q_ref, kband, vband, mask, o_ref)


def _swa_attention(sinks, q, kb, vb, b, s):
    tq = SWA_WINDOW
    nq = s // tq
    nk = SWA_KV_HEADS * SWA_HEAD_DIM
    kb3 = kb.reshape(b, s, nk)
    vb3 = vb.reshape(b, s, nk)
    prev = lambda bi, qi: (bi, jnp.maximum(qi - 1, 0), 0)
    cur = lambda bi, qi: (bi, qi, 0)
    return pl.pallas_call(
        functools.partial(_swa_attn_kernel, tq=tq),
        grid=(b, nq),
        in_specs=[
            pl.BlockSpec(memory_space=pltpu.SMEM),
            pl.BlockSpec((tq, D_MODEL), lambda bi, qi: (bi * nq + qi, 0)),
            pl.BlockSpec((1, tq, nk), prev), pl.BlockSpec((1, tq, nk), cur),
            pl.BlockSpec((1, tq, nk), prev), pl.BlockSpec((1, tq, nk), cur),
        ],
        out_specs=pl.BlockSpec((tq, D_MODEL), lambda bi, qi: (bi * nq + qi, 0)),
        out_shape=jax.ShapeDtypeStruct((b * s, D_MODEL), BF16),
        compiler_params=_params(("parallel", "arbitrary")),
        name="swa_attn",
    )(sinks, q, kb3, kb3, vb3, vb3)


def _swa_attn_sample_kernel(sink_ref, q_ref, ck_ref, cv_ref, kn_ref, vn_ref, o_ref, *, t, past):
    clen = ck_ref.shape[1]
    kband = jnp.concatenate([ck_ref[0].astype(BF16), kn_ref[...]], axis=0)
    vband = jnp.concatenate([cv_ref[0].astype(BF16), vn_ref[...]], axis=0)
    q_pos = past + lax.broadcasted_iota(jnp.int32, (t, 1), 0)
    k_pos = past - clen + lax.broadcasted_iota(jnp.int32, (1, clen + t), 1)
    dc = q_pos // CHUNK - (k_pos + SWA_WINDOW) // CHUNK + SWA_WINDOW // CHUNK
    mask = (dc >= 0) & (dc <= SWA_WINDOW // CHUNK) & (k_pos >= 0)
    _swa_heads(sink_ref, q_ref, kband, vband, mask, o_ref)


def _swa_attention_sample(sinks, q, cache_k, cache_v, kb, vb, layer, db, past):
    n = q.shape[0]
    t = n // db
    clen = cache_k.shape[2]
    nk = SWA_KV_HEADS * SWA_HEAD_DIM
    ck = cache_k.reshape(cache_k.shape[0], db, clen, nk)
    cv = cache_v.reshape(cache_v.shape[0], db, clen, nk)
    row = lambda bi: (bi, 0)
    cache = lambda bi: (layer, bi, 0, 0)
    return pl.pallas_call(
        functools.partial(_swa_attn_sample_kernel, t=t, past=past),
        grid=(db,),
        in_specs=[
            pl.BlockSpec(memory_space=pltpu.SMEM),
            pl.BlockSpec((t, D_MODEL), row),
            pl.BlockSpec((None, 1, clen, nk), cache), pl.BlockSpec((None, 1, clen, nk), cache),
            pl.BlockSpec((t, nk), row), pl.BlockSpec((t, nk), row),
        ],
        out_specs=pl.BlockSpec((t, D_MODEL), row),
        out_shape=jax.ShapeDtypeStruct((n, D_MODEL), BF16),
        compiler_params=_params(("parallel",)),
        name="swa_attn_sample",
    )(sinks, q, ck, cv, kb, vb)


def _post_kernel(x_ref, o_ref, wo_ref, gf_ref, wg_ref, wu_ref, cw_ref, cb_ref, wout_ref, st_ref,
                 p_ref, gp_ref, wpg_ref, wpp_ref, gfin_ref, y_ref, cs_ref, carry_sc, *, final):
    ti = pl.program_id(1)
    ns, tt, _ = x_ref.shape
    rows = ns * tt
    x = x_ref[...].reshape(rows, D_MODEL)
    x = x + _dot(o_ref[...].reshape(rows, o_ref.shape[-1]), wo_ref[...])
    h = _rms(x, gf_ref[...]).astype(BF16)
    t_idx = lax.broadcasted_iota(jnp.int32, (1, tt, 1), 1)

    @pl.when(ti == 0)
    def _():
        carry_sc[...] = st_ref[...]

    proj = _dot(p_ref[...].reshape(rows, PLE_DIM).astype(BF16), wpp_ref[...])
    acc = jnp.zeros((rows, D_MODEL), F32)
    nxt = (_dot(h, wg_ref[0]), _dot(h, wu_ref[0]))
    for c in range(N_FF_CHUNKS):
        sl = slice(c * FF_CHUNK, (c + 1) * FF_CHUNK)
        gate, up = nxt
        if c + 1 < N_FF_CHUNKS:
            nxt = (_dot(h, wg_ref[c + 1]), _dot(h, wu_ref[c + 1]))
        st = carry_sc[:, :, sl]
        g1 = pltpu.roll(gate, 1, 0).reshape(ns, tt, FF_CHUNK)
        g2 = pltpu.roll(gate, 2, 0).reshape(ns, tt, FF_CHUNK)
        gate = gate.reshape(ns, tt, FF_CHUNK)
        g1 = jnp.where(t_idx == 0, st[:, 1:2, :], g1)
        g2 = jnp.where(t_idx == 0, st[:, 0:1, :], jnp.where(t_idx == 1, st[:, 1:2, :], g2))
        cw = cw_ref[:, sl]
        conv = g2 * cw[0:1, :] + g1 * cw[1:2, :] + gate * cw[2:3, :] + cb_ref[:, sl]
        new_st = gate[:, tt - 2:, :]
        carry_sc[:, :, sl] = new_st
        cs_ref[:, :, sl] = new_st
        act = 0.5 * conv * (1.0 + lax.erf(conv * (0.5 ** 0.5)))
        y = (act * up.reshape(ns, tt, FF_CHUNK)).reshape(rows, FF_CHUNK)
        acc = acc + _dot(y.astype(BF16), wout_ref[c])
    x = x + acc
    hp = _rms(x, gp_ref[...]).astype(BF16)
    gate_p = jax.nn.sigmoid(_dot(hp, wpg_ref[...]))
    x = x + gate_p * proj
    if final:
        x = _rms(x, gfin_ref[...])
    y_ref[...] = x.reshape(ns, tt, D_MODEL)


def _post(x3, o2d, w, state, p4, layer, g_final, ns, tt, final):
    bx, tx, _ = x3.shape
    do = o2d.shape[-1]
    o3 = o2d.reshape(bx, tx, do)
    blk = lambda bi, ti: (bi, ti, 0)
    return pl.pallas_call(
        functools.partial(_post_kernel, final=final),
        grid=(bx // ns, tx // tt),
        in_specs=[
            pl.BlockSpec((ns, tt, D_MODEL), blk),
            pl.BlockSpec((ns, tt, do), blk),
            _const_spec(w["wo"].shape),
            _const_spec((1, D_MODEL)),
            _const_spec(w["wg"].shape),
            _const_spec(w["wu"].shape),
            _const_spec(w["cw"].shape),
            _const_spec(w["cb"].shape),
            _const_spec(w["wout"].shape),
            pl.BlockSpec((ns, 2, D_FF), lambda bi, ti: (bi, 0, 0)),
            pl.BlockSpec((None, ns, tt, PLE_DIM), lambda bi, ti: (layer, bi, ti, 0)),
            _const_spec((1, D_MODEL)),
            _const_spec(w["wpg"].shape),
            _const_spec(w["wpp"].shape),
            _const_spec((1, D_MODEL)),
        ],
        out_specs=[pl.BlockSpec((ns, tt, D_MODEL), blk),
                   pl.BlockSpec((ns, 2, D_FF), lambda bi, ti: (bi, 0, 0))],
        out_shape=[jax.ShapeDtypeStruct(x3.shape, F32), jax.ShapeDtypeStruct((bx, 2, D_FF), F32)],
        scratch_shapes=[pltpu.VMEM((ns, 2, D_FF), F32)],
        compiler_params=_params(("parallel", "arbitrary")),
        name="post_ffn",
    )(x3, o3, w["wo"], w["gf"], w["wg"], w["wu"], w["cw"], w["cb"], w["wout"], state, p4,
      w["gp"], w["wpg"], w["wpp"], g_final)


def _rot_cols(w):
    half = w.shape[-1] // 2
    return jnp.concatenate([-w[..., half:], w[..., :half]], axis=-1)


def _rope_tables(pos):
    half = MLA_ROPE // 2
    inv = ROPE_THETA ** (-jnp.arange(half, dtype=F32) / half)
    ang = pos.astype(F32)[:, None] * inv[None, :]
    reps = LANES // half
    return jnp.tile(jnp.cos(ang), (1, reps)), jnp.tile(jnp.sin(ang), (1, reps))


def _prep_mla(w_dq, g_q, w_uq, w_dkv, g_kv, w_uk, w_uv):
    z = jnp.zeros((D_MODEL, LANES - MLA_ROPE), F32)
    kr = w_dkv[:, MLA_KV_LORA:]
    wh = jnp.concatenate([w_dq, w_dkv[:, :MLA_KV_LORA], kr, z, _rot_cols(kr), z], axis=1)
    uq = w_uq.reshape(MLA_Q_LORA, MLA_HEADS, MLA_NOPE + MLA_ROPE)
    rp = uq[:, :, MLA_NOPE:]
    pad = jnp.zeros((MLA_Q_LORA, MLA_HEADS, LANES - MLA_ROPE), F32)
    nh = MLA_HEADS * LANES
    wuq = jnp.concatenate([uq[:, :, :MLA_NOPE].reshape(MLA_Q_LORA, nh),
                           jnp.concatenate([rp, pad], -1).reshape(MLA_Q_LORA, nh),
                           jnp.concatenate([_rot_cols(rp), pad], -1).reshape(MLA_Q_LORA, nh)], axis=1)
    wuq = wuq.astype(BF16)
    return dict(wh=wh.astype(BF16), gq=g_q[None, :], wuq=wuq, wuq_t=wuq.T, gkv=g_kv[None, :],
                wuk=jnp.transpose(w_uk, (1, 2, 0)).astype(BF16),
                wuv=jnp.transpose(w_uv, (1, 0, 2)).astype(BF16),
                wuk_flat=w_uk.reshape(MLA_KV_LORA, MLA_HEADS * MLA_NOPE).astype(BF16),
                wuv_flat_t=w_uv.reshape(MLA_KV_LORA, MLA_HEADS * MLA_V).T.astype(BF16))


def _prep_swa(w_qkv, b_qkv, sinks, w_o):
    nq = SWA_HEADS * SWA_HEAD_DIM
    nk = SWA_KV_HEADS * SWA_HEAD_DIM
    perm = jnp.array(SWA_PERM)

    def split(a):
        lead = a.shape[:-1]
        q = a[..., :nq].reshape(*lead, SWA_HEADS, SWA_HEAD_DIM)[..., perm, :]
        k = a[..., nq:nq + nk].reshape(*lead, SWA_KV_HEADS, SWA_HEAD_DIM)
        return jnp.concatenate([q.reshape(*lead, nq), _rot_cols(q).reshape(*lead, nq),
                                k.reshape(*lead, nk), _rot_cols(k).reshape(*lead, nk), a[..., nq + nk:]], axis=-1)

    wo = w_o.reshape(SWA_HEADS, SWA_HEAD_DIM, D_MODEL)[perm].reshape(nq, D_MODEL)
    return split(w_qkv).astype(BF16), split(b_qkv)[None, :], sinks[perm], wo.astype(BF16)


def _prep_post(w_o, g_ffn, w_in, conv_w, conv_b, w_out, g_ple, w_gate, w_proj):
    def chunks(w):
        return jnp.transpose(w.reshape(D_MODEL, N_FF_CHUNKS, FF_CHUNK), (1, 0, 2)).astype(BF16)
    return dict(wo=w_o.astype(BF16), gf=g_ffn[None, :], wg=chunks(w_in[:, :D_FF]), wu=chunks(w_in[:, D_FF:]),
                cw=conv_w, cb=conv_b[None, :], wout=w_out.reshape(N_FF_CHUNKS, FF_CHUNK, D_MODEL).astype(BF16),
                gp=g_ple[None, :], wpg=w_gate.astype(BF16), wpp=w_proj.astype(BF16))


def kernel(x_prompt, x_sample, p_prompt, p_sample, cache_mla_ckv, cache_mla_krope, cache_sb_k, cache_sb_v, cache_swa_k, cache_swa_v, state_ffn_conv, g_mix, g_ffn, g_ple, g_final, w_mla_dq, g_mla_q, w_mla_uq, w_mla_dkv, g_mla_kv, w_mla_uk, w_mla_uv, w_mla_o, w_sb_qkv, w_sb_o, w_swa_qkv, b_swa_qkv, swa_sinks, w_swa_o, w_ffn_in, ffn_conv_w, ffn_conv_b, w_ffn_out, w_ple_gate, w_ple_proj):
    b, s, _ = x_prompt.shape
    db, t, _ = x_sample.shape
    depth = g_mix.shape[0]
    past = cache_mla_ckv.shape[2]
    n_p = b * s
    n_s = db * t
    assert s % 512 == 0 and t == CHUNK and past % 256 == 0 and n_s % 512 == 0
    assert cache_swa_k.shape[2] == SWA_WINDOW and 512 % t == 0

    tm_mla = 256
    tm = 512
    tt = 512
    ns = 512 // t
    cos_p, sin_p = _rope_tables(jnp.arange(s, dtype=jnp.int32))
    cos_s, sin_s = _rope_tables(past + (jnp.arange(tm, dtype=jnp.int32) % t))
    zero_state = jnp.zeros((b, 2, D_FF), F32)
    g_fin = g_final[None, :]

    xp, xs = x_prompt, x_sample
    outs = {k: [] for k in ("ckv_p", "kr_p", "ckv_s", "kr_s", "sbk_p", "sbv_p", "sbk_s", "sbv_s",
                            "swk_p", "swv_p", "swk_s", "swv_s", "conv_p", "conv_s")}
    for i in range(depth):
        j = i // N_MIXERS
        gm = g_mix[i][None, :]
        xp2 = xp.reshape(n_p, D_MODEL)
        xs2 = xs.reshape(n_s, D_MODEL)
        if i % N_MIXERS == 0:
            w = _prep_mla(w_mla_dq[j], g_mla_q[j], w_mla_uq[j], w_mla_dkv[j], g_mla_kv[j], w_mla_uk[j], w_mla_uv[j])
            q_p, ckv_p, kr_p, k_p, v_p = _mla_project_t(xp2, gm, w, cos_p, sin_p, tm_mla)
            q_s, ckv_s, kr_s, kcat_s = _mla_project(xs2, gm, w, cos_s[:tm_mla], sin_s[:tm_mla], tm_mla)
            o_p = _mla_attention(q_p, k_p, v_p, b, s, 256)
            o_s = _mla_attention_sample(q_s, cache_mla_ckv, cache_mla_krope, kcat_s, w["wuv"], j)
            outs["ckv_p"].append(ckv_p.reshape(b, s, MLA_KV_LORA))
            outs["kr_p"].append(kr_p.reshape(b, s, MLA_ROPE))
            outs["ckv_s"].append(ckv_s.reshape(db, t, MLA_KV_LORA))
            outs["kr_s"].append(kr_s.reshape(db, t, MLA_ROPE))
            w_o = w_mla_o[j]
        elif i % N_MIXERS == 1:
            wq = w_sb_qkv[j].astype(BF16)
            q_t, kb_p, v_t, k_p, v_p = _sb_project_t(xp2, gm, wq, tm)
            qkv_s, k_s, v_s = _sb_project(xs2, gm, wq, tm)
            o_p = _sb_attention(q_t, kb_p, v_t, b, s, 256)
            o_s = _sb_attention_sample(qkv_s, cache_sb_k, cache_sb_v, j, db, 256)
            outs["sbk_p"].append(k_p.reshape(b, s, SB_HEADS, SB_HEAD_DIM))
            outs["sbv_p"].append(v_p.reshape(b, s, SB_HEADS, SB_HEAD_DIM))
            outs["sbk_s"].append(k_s.reshape(db, t, SB_HEADS, SB_HEAD_DIM))
            outs["sbv_s"].append(v_s.reshape(db, t, SB_HEADS, SB_HEAD_DIM))
            w_o = w_sb_o[j]
        else:
            wq, bq, sinks, w_o = _prep_swa(w_swa_qkv[j], b_swa_qkv[j], swa_sinks[j], w_swa_o[j])
            q_p, k_p, v_p, kb_p, vb_p = _swa_project(xp2, gm, wq, bq, cos_p, sin_p, tm)
            q_s, k_s, v_s, kb_s, vb_s = _swa_project(xs2, gm, wq, bq, cos_s, sin_s, tm)
            o_p = _swa_attention(sinks, q_p, kb_p, vb_p, b, s)
            o_s = _swa_attention_sample(sinks, q_s, cache_swa_k, cache_swa_v, kb_s, vb_s, j, db, past)
            kv_shape = (SWA_KV_HEADS, SWA_HEAD_DIM)
            outs["swk_p"].append(k_p.reshape(b, s, *kv_shape)[:, -SWA_WINDOW:])
            outs["swv_p"].append(v_p.reshape(b, s, *kv_shape)[:, -SWA_WINDOW:])
            outs["swk_s"].append(jnp.concatenate([cache_swa_k[j], k_s.reshape(db, t, *kv_shape)], 1)[:, -SWA_WINDOW:])
            outs["swv_s"].append(jnp.concatenate([cache_swa_v[j], v_s.reshape(db, t, *kv_shape)], 1)[:, -SWA_WINDOW:])
        wp = _prep_post(w_o, g_ffn[i], w_ffn_in[i], ffn_conv_w[i], ffn_conv_b[i], w_ffn_out[i],
                        g_ple[i], w_ple_gate[i], w_ple_proj[i])
        final = i == depth - 1
        xp, cp = _post(xp, o_p, wp, zero_state, p_prompt, i, g_fin, 1, tt, final)
        xs, cs = _post(xs, o_s, wp, state_ffn_conv[i], p_sample, i, g_fin, ns, t, final)
        outs["conv_p"].append(cp)
        outs["conv_s"].append(cs)
    order = ("ckv_p", "kr_p", "ckv_s", "kr_s", "sbk_p", "sbv_p", "sbk_s", "sbv_s",
             "swk_p", "swv_p", "swk_s", "swv_s", "conv_p", "conv_s")
    return (xp, xs) + tuple(jnp.stack(outs[k]) for k in order)
```

```python
import functools

import jax
import jax.numpy as jnp
from jax import lax
from jax.experimental import pallas as pl
from jax.experimental.pallas import tpu as pltpu

F32 = jnp.float32
BF16 = jnp.bfloat16

D_MODEL = 1024
CHUNK = 64
N_MIXERS = 3
RMS_EPS = 1e-6
ROPE_THETA = 10000.0
NEG = -1e30
LOG2E = 1.4426950408889634

MLA_HEADS = 16
MLA_NOPE = 128
MLA_ROPE = 64
MLA_V = 128
MLA_Q_LORA = 384
MLA_KV_LORA = 256
MLA_SCALE = (MLA_NOPE + MLA_ROPE) ** -0.5
MLA_QK = MLA_KV_LORA + 128
MLA_KDIM = MLA_NOPE + 128
MLA_HEAD_GROUP = 8

SB_HEADS = 16
SB_HEAD_DIM = 64
SB_SCALE = SB_HEAD_DIM ** -0.5

SWA_HEADS = 16
SWA_KV_HEADS = 4
SWA_GROUP = SWA_HEADS // SWA_KV_HEADS
SWA_HEAD_DIM = 64
SWA_WINDOW = 128
SWA_SCALE = SWA_HEAD_DIM ** -0.5
SWA_PERM = tuple(8 * p + 4 * s + g for p in range(2) for g in range(SWA_GROUP) for s in range(2))

D_FF = 2816
FF_CHUNK = 256
N_FF_CHUNKS = D_FF // FF_CHUNK
PLE_DIM = 256

LANES = 128
VMEM_LIMIT = 56 * 1024 * 1024


def _dot(a, b):
    return jnp.dot(a, b, preferred_element_type=F32)


def _dot_nt(a, b):
    return lax.dot_general(a, b, (((1,), (1,)), ((), ())), preferred_element_type=F32)


def _rms(x, g):
    ms = jnp.mean(x * x, axis=-1, keepdims=True)
    return x * lax.rsqrt(ms + RMS_EPS) * g


def _const_spec(shape):
    nd = len(shape)
    return pl.BlockSpec(shape, lambda *_: (0,) * nd, pipeline_mode=pl.Buffered(1))


def _params(sem):
    return pltpu.CompilerParams(dimension_semantics=sem, vmem_limit_bytes=VMEM_LIMIT)


def _mla_latents(x_ref, g_ref, wh_ref, gq_ref, gkv_ref, cos, sin, ckv_ref, kr_ref, kcat_ref):
    h = _rms(x_ref[...], g_ref[...]).astype(BF16)
    y = _dot(h, wh_ref[...])
    cq = _rms(y[:, :MLA_Q_LORA], gq_ref[...])
    o = MLA_Q_LORA
    ckv = _rms(y[:, o:o + MLA_KV_LORA], gkv_ref[...])
    o += MLA_KV_LORA
    kr = y[:, o:o + LANES] * cos + y[:, o + LANES:o + 2 * LANES] * sin
    ckv_ref[...] = ckv
    kr_ref[...] = kr[:, :MLA_ROPE]
    if kcat_ref is not None:
        kcat_ref[...] = jnp.concatenate([ckv, kr], axis=-1).astype(BF16)
    return cq, ckv, kr


def _mla_proj_t_kernel(x_ref, g_ref, wh_ref, gq_ref, wuq_ref, wuk_ref, wuv_ref, gkv_ref, cos_ref, sin_ref,
                       cos_t_ref, sin_t_ref, q_ref, ckv_ref, kr_ref, k_ref, v_ref):
    cq, ckv, kr = _mla_latents(x_ref, g_ref, wh_ref, gq_ref, gkv_ref, cos_ref[...], sin_ref[...],
                               ckv_ref, kr_ref, None)
    ckv_b = ckv.astype(BF16)
    kn = _dot(ckv_b, wuk_ref[...])
    kr_b = kr.astype(BF16)
    for hd in range(MLA_HEADS):
        k_ref[:, hd * MLA_KDIM:hd * MLA_KDIM + LANES] = kn[:, hd * LANES:(hd + 1) * LANES].astype(BF16)
        k_ref[:, hd * MLA_KDIM + LANES:(hd + 1) * MLA_KDIM] = kr_b
    v_ref[...] = _dot(wuv_ref[...], ckv.T.astype(BF16)).astype(BF16)
    qa = _dot(wuq_ref[...], cq.T.astype(BF16))
    cos = cos_t_ref[...]
    sin = sin_t_ref[...]
    nh = MLA_HEADS * LANES
    for hd in range(MLA_HEADS):
        sl = slice(hd * LANES, (hd + 1) * LANES)
        qr = qa[nh:2 * nh][sl] * cos + qa[2 * nh:][sl] * sin
        q_ref[hd] = (jnp.concatenate([qa[sl], qr], axis=0) * (MLA_SCALE * LOG2E)).astype(BF16)


def _mla_project_t(x2d, g, w, cos, sin, tm):
    n = x2d.shape[0]
    ntab = cos.shape[0] // tm
    row = lambda i: (i, 0)
    col = lambda i: (0, i)
    return pl.pallas_call(
        _mla_proj_t_kernel,
        grid=(n // tm,),
        in_specs=[
            pl.BlockSpec((tm, D_MODEL), row),
            _const_spec((1, D_MODEL)),
            _const_spec(w["wh"].shape),
            _const_spec((1, MLA_Q_LORA)),
            _const_spec(w["wuq_t"].shape),
            _const_spec(w["wuk_flat"].shape),
            _const_spec(w["wuv_flat_t"].shape),
            _const_spec((1, MLA_KV_LORA)),
            pl.BlockSpec((tm, LANES), lambda i: (i % ntab, 0)),
            pl.BlockSpec((tm, LANES), lambda i: (i % ntab, 0)),
            pl.BlockSpec((LANES, tm), lambda i: (0, i % ntab)),
            pl.BlockSpec((LANES, tm), lambda i: (0, i % ntab)),
        ],
        out_specs=[
            pl.BlockSpec((MLA_HEADS, MLA_KDIM, tm), lambda i: (0, 0, i)),
            pl.BlockSpec((tm, MLA_KV_LORA), row),
            pl.BlockSpec((tm, MLA_ROPE), row),
            pl.BlockSpec((tm, MLA_HEADS * MLA_KDIM), row),
            pl.BlockSpec((MLA_HEADS * MLA_V, tm), col),
        ],
        out_shape=[
            jax.ShapeDtypeStruct((MLA_HEADS, MLA_KDIM, n), BF16),
            jax.ShapeDtypeStruct((n, MLA_KV_LORA), F32),
            jax.ShapeDtypeStruct((n, MLA_ROPE), F32),
            jax.ShapeDtypeStruct((n, MLA_HEADS * MLA_KDIM), BF16),
            jax.ShapeDtypeStruct((MLA_HEADS * MLA_V, n), BF16),
        ],
        compiler_params=_params(("parallel",)),
        name="mla_proj_t",
    )(x2d, g, w["wh"], w["gq"], w["wuq_t"], w["wuk_flat"], w["wuv_flat_t"], w["gkv"], cos, sin, cos.T, sin.T)


def _mla_proj_kernel(x_ref, g_ref, wh_ref, gq_ref, wuq_ref, wuk_ref, gkv_ref, cos_ref, sin_ref,
                     q_ref, ckv_ref, kr_ref, kcat_ref):
    cos = cos_ref[...]
    sin = sin_ref[...]
    cq, _, _ = _mla_latents(x_ref, g_ref, wh_ref, gq_ref, gkv_ref, cos, sin, ckv_ref, kr_ref, kcat_ref)
    qa = _dot(cq.astype(BF16), wuq_ref[...])
    nh = MLA_HEADS * LANES
    for hd in range(MLA_HEADS):
        sl = slice(hd * LANES, (hd + 1) * LANES)
        ql = _dot(qa[:, sl].astype(BF16), wuk_ref[hd])
        qr = qa[:, nh:2 * nh][:, sl] * cos + qa[:, 2 * nh:][:, sl] * sin
        q_ref[hd] = (jnp.concatenate([ql, qr], axis=-1) * MLA_SCALE).astype(BF16)


def _mla_project(x2d, g, w, cos, sin, tm):
    n = x2d.shape[0]
    ntab = cos.shape[0] // tm
    row = lambda i: (i, 0)
    tab = lambda i: (i % ntab, 0)
    return pl.pallas_call(
        _mla_proj_kernel,
        grid=(n // tm,),
        in_specs=[
            pl.BlockSpec((tm, D_MODEL), row),
            _const_spec((1, D_MODEL)),
            _const_spec(w["wh"].shape),
            _const_spec((1, MLA_Q_LORA)),
            _const_spec(w["wuq"].shape),
            _const_spec(w["wuk"].shape),
            _const_spec((1, MLA_KV_LORA)),
            pl.BlockSpec((tm, LANES), tab),
            pl.BlockSpec((tm, LANES), tab),
        ],
        out_specs=[
            pl.BlockSpec((MLA_HEADS, tm, MLA_QK), lambda i: (0, i, 0)),
            pl.BlockSpec((tm, MLA_KV_LORA), row),
            pl.BlockSpec((tm, MLA_ROPE), row),
            pl.BlockSpec((tm, MLA_QK), row),
        ],
        out_shape=[
            jax.ShapeDtypeStruct((MLA_HEADS, n, MLA_QK), BF16),
            jax.ShapeDtypeStruct((n, MLA_KV_LORA), F32),
            jax.ShapeDtypeStruct((n, MLA_ROPE), F32),
            jax.ShapeDtypeStruct((n, MLA_QK), BF16),
        ],
        compiler_params=_params(("parallel",)),
        name="mla_proj",
    )(x2d, g, w["wh"], w["gq"], w["wuq"], w["wuk"], w["gkv"], cos, sin)


def _sb_proj_kernel(x_ref, g_ref, w_ref, qkv_ref, k_ref, v_ref):
    h = _rms(x_ref[...], g_ref[...]).astype(BF16)
    y = _dot(h, w_ref[...])
    k_ref[...] = y[:, D_MODEL:2 * D_MODEL]
    v_ref[...] = y[:, 2 * D_MODEL:]
    qkv_ref[:, :D_MODEL] = (y[:, :D_MODEL] * (SB_SCALE * LOG2E)).astype(BF16)
    qkv_ref[:, D_MODEL:] = y[:, D_MODEL:].astype(BF16)


def _sb_proj_t_kernel(x_ref, g_ref, w_ref, q_t_ref, kb_ref, v_t_ref, k_t_ref, vf_t_ref):
    h = _rms(x_ref[...], g_ref[...]).astype(BF16)
    y = _dot(h, w_ref[...])
    k = y[:, D_MODEL:2 * D_MODEL]
    v_t = y[:, 2 * D_MODEL:].T
    k_t_ref[0] = k.T
    vf_t_ref[0] = v_t
    kb_ref[...] = k.astype(BF16)
    q_t_ref[...] = (y[:, :D_MODEL] * (SB_SCALE * LOG2E)).T.astype(BF16)
    v_t_ref[...] = v_t.astype(BF16)


def _sb_project_t(x2d, g, w, b, tm):
    n = x2d.shape[0]
    nt = n // b // tm
    row = lambda i: (i, 0)
    col = lambda i: (0, i)
    per_batch = lambda i: (i // nt, 0, i % nt)
    return pl.pallas_call(
        _sb_proj_t_kernel,
        grid=(n // tm,),
        in_specs=[pl.BlockSpec((tm, D_MODEL), row), _const_spec((1, D_MODEL)), _const_spec(w.shape)],
        out_specs=[pl.BlockSpec((D_MODEL, tm), col), pl.BlockSpec((tm, D_MODEL), row),
                   pl.BlockSpec((D_MODEL, tm), col), pl.BlockSpec((1, D_MODEL, tm), per_batch),
                   pl.BlockSpec((1, D_MODEL, tm), per_batch)],
        out_shape=[jax.ShapeDtypeStruct((D_MODEL, n), BF16), jax.ShapeDtypeStruct((n, D_MODEL), BF16),
                   jax.ShapeDtypeStruct((D_MODEL, n), BF16), jax.ShapeDtypeStruct((b, D_MODEL, n // b), F32),
                   jax.ShapeDtypeStruct((b, D_MODEL, n // b), F32)],
        compiler_params=_params(("parallel",)),
        name="sb_proj_t",
    )(x2d, g, w)


def _sb_project(x2d, g, w, tm):
    n = x2d.shape[0]
    row = lambda i: (i, 0)
    return pl.pallas_call(
        _sb_proj_kernel,
        grid=(n // tm,),
        in_specs=[pl.BlockSpec((tm, D_MODEL), row), _const_spec((1, D_MODEL)), _const_spec(w.shape)],
        out_specs=[pl.BlockSpec((tm, 3 * D_MODEL), row), pl.BlockSpec((tm, D_MODEL), row),
                   pl.BlockSpec((tm, D_MODEL), row)],
        out_shape=[jax.ShapeDtypeStruct((n, 3 * D_MODEL), BF16), jax.ShapeDtypeStruct((n, D_MODEL), F32),
                   jax.ShapeDtypeStruct((n, D_MODEL), F32)],
        compiler_params=_params(("parallel",)),
        name="sb_proj",
    )(x2d, g, w)


def _swa_proj_kernel(x_ref, g_ref, w_ref, b_ref, cos_ref, sin_ref, q_ref, k_ref, v_ref, kb_ref, vb_ref):
    h = _rms(x_ref[...], g_ref[...]).astype(BF16)
    y = _dot(h, w_ref[...]) + b_ref[...]
    cos = cos_ref[...]
    sin = sin_ref[...]
    nq = SWA_HEADS * SWA_HEAD_DIM
    nk = SWA_KV_HEADS * SWA_HEAD_DIM
    for c in range(nq // LANES):
        sl = slice(c * LANES, (c + 1) * LANES)
        q = y[:, :nq][:, sl] * cos + y[:, nq:2 * nq][:, sl] * sin
        q_ref[:, sl] = (q * SWA_SCALE).astype(BF16)
    o = 2 * nq
    for c in range(nk // LANES):
        sl = slice(c * LANES, (c + 1) * LANES)
        k = y[:, o:o + nk][:, sl] * cos + y[:, o + nk:o + 2 * nk][:, sl] * sin
        k_ref[:, sl] = k
        kb_ref[:, sl] = k.astype(BF16)
    v = y[:, o + 2 * nk:]
    v_ref[...] = v
    vb_ref[...] = v.astype(BF16)


def _swa_project(x2d, g, w, b, cos, sin, tm):
    n = x2d.shape[0]
    ntab = cos.shape[0] // tm
    nq = SWA_HEADS * SWA_HEAD_DIM
    nk = SWA_KV_HEADS * SWA_HEAD_DIM
    row = lambda i: (i, 0)
    tab = lambda i: (i % ntab, 0)
    return pl.pallas_call(
        _swa_proj_kernel,
        grid=(n // tm,),
        in_specs=[pl.BlockSpec((tm, D_MODEL), row), _const_spec((1, D_MODEL)), _const_spec(w.shape),
                  _const_spec(b.shape), pl.BlockSpec((tm, LANES), tab), pl.BlockSpec((tm, LANES), tab)],
        out_specs=[pl.BlockSpec((tm, nq), row), pl.BlockSpec((tm, nk), row), pl.BlockSpec((tm, nk), row),
                   pl.BlockSpec((tm, nk), row), pl.BlockSpec((tm, nk), row)],
        out_shape=[jax.ShapeDtypeStruct((n, nq), BF16), jax.ShapeDtypeStruct((n, nk), F32),
                   jax.ShapeDtypeStruct((n, nk), F32), jax.ShapeDtypeStruct((n, nk), BF16),
                   jax.ShapeDtypeStruct((n, nk), BF16)],
        compiler_params=_params(("parallel",)),
        name="swa_proj",
    )(x2d, g, w, b, cos, sin)


def _mla_attn_kernel(q_ref, k_ref, v_ref, o_ref, m_sc, l_sc, acc_sc, *, t):
    qi = pl.program_id(2)
    nh = q_ref.shape[0]
    m_sc[...] = jnp.full(m_sc.shape, NEG, F32)
    l_sc[...] = jnp.zeros(l_sc.shape, F32)
    acc_sc[...] = jnp.zeros(acc_sc.shape, F32)
    k_chunk = lax.broadcasted_iota(jnp.int32, (t, 1), 0) // CHUNK
    q_chunk = lax.broadcasted_iota(jnp.int32, (1, t), 1) // CHUNK
    diag_mask = k_chunk <= q_chunk

    def step(j, mask):
        start = pl.multiple_of(j * t, t)

        def scores(hd):
            return _dot(k_ref[0, pl.ds(start, t), hd * MLA_KDIM:(hd + 1) * MLA_KDIM], q_ref[hd])

        ahead = 3
        pending = [scores(hd) for hd in range(ahead)]
        for hd in range(nh):
            s = pending.pop(0)
            if hd + ahead < nh:
                pending.append(scores(hd + ahead))
            if mask is not None:
                s = jnp.where(mask, s, NEG)
            m_old = m_sc[hd]
            m_new = jnp.maximum(m_old, jnp.max(s, axis=0, keepdims=True))
            alpha = jnp.exp2(m_old - m_new)
            p = jnp.exp2(s - m_new)
            l_sc[hd] = alpha * l_sc[hd] + jnp.sum(p, axis=0, keepdims=True)
            v = v_ref[hd * MLA_V:(hd + 1) * MLA_V, pl.ds(start, t)]
            acc_sc[hd] = alpha * acc_sc[hd] + _dot(v, p.astype(BF16))
            m_sc[hd] = m_new

    def body(j, _):
        step(j, None)
        return 0

    lax.fori_loop(0, qi, body, 0)
    step(qi, diag_mask)
    for hd in range(nh):
        o_ref[:, hd * MLA_V:(hd + 1) * MLA_V] = (acc_sc[hd] / l_sc[hd]).T.astype(BF16)


def _mla_attention(q_t, k_full, v_t, b, s, t):
    n = b * s
    nq = s // t
    g = MLA_HEAD_GROUP
    return pl.pallas_call(
        functools.partial(_mla_attn_kernel, t=t),
        grid=(b, MLA_HEADS // g, nq),
        in_specs=[
            pl.BlockSpec((g, MLA_KDIM, t), lambda bi, hg, qi: (hg, 0, bi * nq + qi)),
            pl.BlockSpec((1, s, g * MLA_KDIM), lambda bi, hg, qi: (bi, 0, hg)),
            pl.BlockSpec((g * MLA_V, s), lambda bi, hg, qi: (hg, bi)),
        ],
        out_specs=pl.BlockSpec((t, g * MLA_V), lambda bi, hg, qi: (bi * nq + qi, hg)),
        out_shape=jax.ShapeDtypeStruct((n, MLA_HEADS * MLA_V), BF16),
        scratch_shapes=[pltpu.VMEM((g, 1, t), F32), pltpu.VMEM((g, 1, t), F32),
                        pltpu.VMEM((g, MLA_V, t), F32)],
        compiler_params=_params(("parallel", "parallel", "arbitrary")),
        name="mla_attn",
    )(q_t, k_full.reshape(b, s, MLA_HEADS * MLA_KDIM), v_t)


def _mla_attn_sample_kernel(q_ref, cc_ref, cr_ref, kn_ref, wuv_ref, o_ref, *, t, past):
    rows = MLA_HEADS * t
    q = q_ref[...].reshape(rows, MLA_QK)
    ck = cc_ref[0].astype(BF16)
    s_old = (_dot_nt(q[:, :MLA_KV_LORA], ck)
             + _dot(q[:, MLA_KV_LORA:MLA_KV_LORA + MLA_ROPE], cr_ref[0].astype(BF16)))
    kn = kn_ref[...]
    s_new = _dot_nt(q, kn)
    tok = lax.broadcasted_iota(jnp.int32, (rows, 1), 0) & (t - 1)
    k_chunk = (past + lax.broadcasted_iota(jnp.int32, (1, t), 1)) // CHUNK
    s_new = jnp.where(k_chunk <= (past + tok) // CHUNK, s_new, NEG)
    m = jnp.maximum(jnp.max(s_old, axis=-1, keepdims=True), jnp.max(s_new, axis=-1, keepdims=True))
    p_old = jnp.exp(s_old - m)
    p_new = jnp.exp(s_new - m)
    l = jnp.sum(p_old, axis=-1, keepdims=True) + jnp.sum(p_new, axis=-1, keepdims=True)
    acc = _dot(p_old.astype(BF16), ck) + _dot(p_new.astype(BF16), kn[:, :MLA_KV_LORA])
    o_lat = (acc / l).astype(BF16)
    for hd in range(MLA_HEADS):
        o = _dot(o_lat[hd * t:(hd + 1) * t], wuv_ref[hd])
        o_ref[:, hd * MLA_V:(hd + 1) * MLA_V] = o.astype(BF16)


def _mla_attention_sample(q, cache_ckv, cache_kr, kcat, wuv, layer):
    _, db, past, _ = cache_ckv.shape
    t = q.shape[1] // db
    return pl.pallas_call(
        functools.partial(_mla_attn_sample_kernel, t=t, past=past),
        grid=(db,),
        in_specs=[
            pl.BlockSpec((MLA_HEADS, t, MLA_QK), lambda bi: (0, bi, 0)),
            pl.BlockSpec((None, 1, past, MLA_KV_LORA), lambda bi: (layer, bi, 0, 0)),
            pl.BlockSpec((None, 1, MLA_ROPE, past), lambda bi: (layer, bi, 0, 0)),
            pl.BlockSpec((t, MLA_QK), lambda bi: (bi, 0)),
            _const_spec(wuv.shape),
        ],
        out_specs=pl.BlockSpec((t, MLA_HEADS * MLA_V), lambda bi: (bi, 0)),
        out_shape=jax.ShapeDtypeStruct((db * t, MLA_HEADS * MLA_V), BF16),
        compiler_params=_params(("parallel",)),
        name="mla_attn_sample",
    )(q, cache_ckv, jnp.swapaxes(cache_kr, 2, 3), kcat, wuv)


def _later_key_matrix(tk):
    r = lax.broadcasted_iota(jnp.int32, (tk, tk), 0)
    c = lax.broadcasted_iota(jnp.int32, (tk, tk), 1)
    return jnp.where(r > c, 1.0, 0.0).astype(BF16)


def _split_pair(q2):
    lane = lax.broadcasted_iota(jnp.int32, (1, LANES), 1)
    zero = jnp.zeros_like(q2)
    return jnp.where(lane < LANES // 2, q2, zero), jnp.where(lane >= LANES // 2, q2, zero), lane


def _sb_logits_stage(z, mask):
    log_sig = jnp.minimum(z, 0.0) - jnp.log2(1.0 + jnp.exp2(jnp.minimum(z, -z)))
    log_not = log_sig - z
    if mask is not None:
        log_not = jnp.where(mask, log_not, 0.0)
    hi = log_not.astype(BF16)
    lo = (log_not - hi.astype(F32)).astype(BF16)
    return log_sig, log_not, hi, lo


def _sb_attn_kernel(q_ref, k_ref, v_ref, o_ref, carry_sc, acc_sc, *, t):
    qi = pl.program_id(1)
    low = lax.broadcasted_iota(jnp.int32, (LANES, 1), 0) < LANES // 2
    r = lax.broadcasted_iota(jnp.int32, (t, t), 0)
    c = lax.broadcasted_iota(jnp.int32, (t, t), 1)
    later = jnp.where(c > r, 1.0, 0.0).astype(BF16)
    later2 = jnp.concatenate([later, later], axis=1)
    diag_mask = r < c
    carry_sc[...] = jnp.zeros(carry_sc.shape, F32)
    acc_sc[...] = jnp.zeros(acc_sc.shape, F32)

    def pick(x, hd):
        zero = jnp.zeros_like(x)
        return jnp.where(low, x, zero) if hd % 2 == 0 else jnp.where(low, zero, x)

    def block(j, mask):
        start = pl.multiple_of(j * t, t)

        def logits(hd):
            g = hd // 2
            qh = pick(q_ref[g * LANES:(g + 1) * LANES, :], hd)
            return _dot(k_ref[0, pl.ds(start, t), g * LANES:(g + 1) * LANES], qh)

        def stage1(z):
            log_sig, log_not, hi, lo = _sb_logits_stage(z, mask)
            tail = _dot(later2, jnp.concatenate([hi, lo], axis=0))
            return log_sig, tail, jnp.sum(log_not, axis=0, keepdims=True)

        def stage2(hd, log_sig, tail, colsum):
            carry = carry_sc[hd]
            a = jnp.exp2(log_sig + (tail + carry))
            if mask is not None:
                a = jnp.where(mask, a, 0.0)
            carry_sc[hd] = carry + colsum
            g = hd // 2
            vh = pick(v_ref[g * LANES:(g + 1) * LANES, pl.ds(start, t)], hd)
            acc_sc[g] += _dot(vh, a.astype(BF16))

        ahead = 2
        zs = [logits(hd) for hd in range(ahead)]
        pending = None
        for hd in range(SB_HEADS):
            st = stage1(zs.pop(0))
            if hd + ahead < SB_HEADS:
                zs.append(logits(hd + ahead))
            if pending is not None:
                stage2(hd - 1, *pending)
            pending = st
        stage2(SB_HEADS - 1, *pending)

    block(qi, diag_mask)

    def body(jj, _):
        block(qi - 1 - jj, None)
        return 0

    lax.fori_loop(0, qi, body, 0)
    for g in range(SB_HEADS // 2):
        o_ref[:, g * LANES:(g + 1) * LANES] = acc_sc[g].T.astype(BF16)


def _sb_attention(q_t, kb, v_t, b, s, t):
    n = b * s
    nq = s // t
    return pl.pallas_call(
        functools.partial(_sb_attn_kernel, t=t),
        grid=(b, nq),
        in_specs=[
            pl.BlockSpec((D_MODEL, t), lambda bi, qi: (0, bi * nq + qi)),
            pl.BlockSpec((1, s, D_MODEL), lambda bi, qi: (bi, 0, 0)),
            pl.BlockSpec((D_MODEL, s), lambda bi, qi: (0, bi)),
        ],
        out_specs=pl.BlockSpec((t, D_MODEL), lambda bi, qi: (bi * nq + qi, 0)),
        out_shape=jax.ShapeDtypeStruct((n, D_MODEL), BF16),
        scratch_shapes=[pltpu.VMEM((SB_HEADS, 1, t), F32), pltpu.VMEM((SB_HEADS // 2, LANES, t), F32)],
        compiler_params=_params(("parallel", "arbitrary")),
        name="sb_attn",
    )(q_t, kb.reshape(b, s, D_MODEL), v_t)


def _sb_attn_sample_kernel(q_ref, kn_ref, vn_ref, ck_ref, cv_ref, o_ref, carry_sc, acc_sc, *, t, tk):
    j = pl.program_id(1)
    hdim = SB_HEAD_DIM

    def head(ref, hd):
        return ref[:, hd * hdim:(hd + 1) * hdim]

    def run(logits, weighted, later2, mask, first):
        def stage1(z):
            log_sig, log_not, hi, lo = _sb_logits_stage(z, mask)
            tail = _dot(jnp.concatenate([hi, lo], axis=1), later2)
            return log_sig, tail, jnp.sum(log_not, axis=-1, keepdims=True)

        def stage2(hd, log_sig, tail, rowsum):
            if first:
                a = jnp.exp2(log_sig + tail)
                carry_sc[hd] = rowsum
            else:
                carry = carry_sc[hd]
                a = jnp.exp2(log_sig + (tail + carry))
                carry_sc[hd] = carry + rowsum
            if mask is not None:
                a = jnp.where(mask, a, 0.0)
            o = weighted(hd, a.astype(BF16))
            acc_sc[hd] = o if first else acc_sc[hd] + o

        ahead, lag = 4, 2
        zs = [logits(hd) for hd in range(ahead)]
        pending = []
        for hd in range(SB_HEADS):
            pending.append((hd,) + stage1(zs.pop(0)))
            if hd + ahead < SB_HEADS:
                zs.append(logits(hd + ahead))
            if len(pending) > lag:
                stage2(*pending.pop(0))
        for item in pending:
            stage2(*item)

    def later_pair(n):
        u = _later_key_matrix(n)
        return jnp.concatenate([u, u], axis=0)

    @pl.when(j == 0)
    def _():
        r = lax.broadcasted_iota(jnp.int32, (t, t), 0)
        c = lax.broadcasted_iota(jnp.int32, (t, t), 1)
        run(lambda hd: _dot_nt(head(q_ref, hd), head(kn_ref, hd)),
            lambda hd, a: _dot(a, head(vn_ref, hd)), later_pair(t), c < r, True)

    run(lambda hd: _dot(head(q_ref, hd), ck_ref[0, hd].astype(BF16)),
        lambda hd, a: _dot_nt(a, cv_ref[0, hd].astype(BF16)), later_pair(tk), None, False)

    @pl.when(j == pl.num_programs(1) - 1)
    def _():
        o_ref[...] = jnp.concatenate([acc_sc[hd] for hd in range(SB_HEADS)], axis=-1).astype(BF16)


def _sb_attention_sample(qkv, cache_k, cache_v, layer, db, tk):
    n = qkv.shape[0]
    t = n // db
    past = cache_k.shape[2]
    nblk = past // tk
    cache = lambda bi, j: (layer, bi, 0, 0, nblk - 1 - j)
    cache_k = jnp.transpose(cache_k, (0, 1, 3, 4, 2))
    cache_v = jnp.transpose(cache_v, (0, 1, 3, 4, 2))
    return pl.pallas_call(
        functools.partial(_sb_attn_sample_kernel, t=t, tk=tk),
        grid=(db, nblk),
        in_specs=[
            pl.BlockSpec((t, D_MODEL), lambda bi, j: (bi, 0)),
            pl.BlockSpec((t, D_MODEL), lambda bi, j: (bi, 1)),
            pl.BlockSpec((t, D_MODEL), lambda bi, j: (bi, 2)),
            pl.BlockSpec((None, 1, SB_HEADS, SB_HEAD_DIM, tk), cache),
            pl.BlockSpec((None, 1, SB_HEADS, SB_HEAD_DIM, tk), cache),
        ],
        out_specs=pl.BlockSpec((t, D_MODEL), lambda bi, j: (bi, 0)),
        out_shape=jax.ShapeDtypeStruct((n, D_MODEL), BF16),
        scratch_shapes=[pltpu.VMEM((SB_HEADS, t, 1), F32), pltpu.VMEM((SB_HEADS, t, SB_HEAD_DIM), F32)],
        compiler_params=_params(("parallel", "arbitrary")),
        name="sb_attn_sample",
    )(qkv, qkv, qkv, cache_k, cache_v)


def _swa_heads(sink_ref, q_ref, kband, vband, mask, o_ref):
    lane = lax.broadcasted_iota(jnp.int32, (1, LANES), 1)

    def scores(hd):
        blk, half = divmod(hd, 2)
        pair = blk // SWA_GROUP
        qh = _split_pair(q_ref[:, blk * LANES:(blk + 1) * LANES])[half]
        return _dot_nt(qh, kband[:, pair * LANES:(pair + 1) * LANES])

    ahead = 3
    pending = [scores(hd) for hd in range(ahead)]
    outs = []
    for hd in range(SWA_HEADS):
        blk, half = divmod(hd, 2)
        pair = blk // SWA_GROUP
        s = jnp.where(mask, pending.pop(0), NEG)
        if hd + ahead < SWA_HEADS:
            pending.append(scores(hd + ahead))
        sink = sink_ref[hd]
        m = jnp.maximum(jnp.max(s, axis=-1, keepdims=True), sink)
        e = jnp.exp(s - m)
        den = jnp.sum(e, axis=-1, keepdims=True) + jnp.exp(sink - m)
        outs.append(_dot((e / den).astype(BF16), vband[:, pair * LANES:(pair + 1) * LANES]))
        if half:
            o_ref[:, blk * LANES:(blk + 1) * LANES] = jnp.where(lane < LANES // 2, outs[-2], outs[-1]).astype(BF16)


def _swa_attn_kernel(sink_ref, q_ref, kp_ref, kc_ref, vp_ref, vc_ref, o_ref, *, tq):
    i = pl.program_id(1)
    kband = jnp.concatenate([kp_ref[0], kc_ref[0]], axis=0)
    vband = jnp.concatenate([vp_ref[0], vc_ref[0]], axis=0)
    q_pos = i * tq + lax.broadcasted_iota(jnp.int32, (tq, 1), 0)
    k_pos = i * tq - SWA_WINDOW + lax.broadcasted_iota(jnp.int32, (1, SWA_WINDOW + tq), 1)
    dc = (q_pos + SWA_WINDOW) // CHUNK - (k_pos + SWA_WINDOW) // CHUNK
    mask = (dc >= 0) & (dc <= SWA_WINDOW // CHUNK) & (k_pos >= 0)
    _swa_heads(sink_ref, q_ref, kband, vband, mask, o_ref)


def _swa_attention(sinks, q, kb, vb, b, s):
    tq = SWA_WINDOW
    nq = s // tq
    nk = SWA_KV_HEADS * SWA_HEAD_DIM
    kb3 = kb.reshape(b, s, nk)
    vb3 = vb.reshape(b, s, nk)
    prev = lambda bi, qi: (bi, jnp.maximum(qi - 1, 0), 0)
    cur = lambda bi, qi: (bi, qi, 0)
    return pl.pallas_call(
        functools.partial(_swa_attn_kernel, tq=tq),
        grid=(b, nq),
        in_specs=[
            pl.BlockSpec(memory_space=pltpu.SMEM),
            pl.BlockSpec((tq, D_MODEL), lambda bi, qi: (bi * nq + qi, 0)),
            pl.BlockSpec((1, tq, nk), prev), pl.BlockSpec((1, tq, nk), cur),
            pl.BlockSpec((1, tq, nk), prev), pl.BlockSpec((1, tq, nk), cur),
        ],
        out_specs=pl.BlockSpec((tq, D_MODEL), lambda bi, qi: (bi * nq + qi, 0)),
        out_shape=jax.ShapeDtypeStruct((b * s, D_MODEL), BF16),
        compiler_params=_params(("parallel", "arbitrary")),
        name="swa_attn",
    )(sinks, q, kb3, kb3, vb3, vb3)


def _swa_attn_sample_kernel(sink_ref, q_ref, ck_ref, cv_ref, kn_ref, vn_ref, o_ref, *, t, past):
    clen = ck_ref.shape[1]
    kband = jnp.concatenate([ck_ref[0].astype(BF16), kn_ref[...]], axis=0)
    vband = jnp.concatenate([cv_ref[0].astype(BF16), vn_ref[...]], axis=0)
    q_pos = past + lax.broadcasted_iota(jnp.int32, (t, 1), 0)
    k_pos = past - clen + lax.broadcasted_iota(jnp.int32, (1, clen + t), 1)
    dc = q_pos // CHUNK - (k_pos + SWA_WINDOW) // CHUNK + SWA_WINDOW // CHUNK
    mask = (dc >= 0) & (dc <= SWA_WINDOW // CHUNK) & (k_pos >= 0)
    _swa_heads(sink_ref, q_ref, kband, vband, mask, o_ref)


def _swa_attention_sample(sinks, q, cache_k, cache_v, kb, vb, layer, db, past):
    n = q.shape[0]
    t = n // db
    clen = cache_k.shape[2]
    nk = SWA_KV_HEADS * SWA_HEAD_DIM
    ck = cache_k.reshape(cache_k.shape[0], db, clen, nk)
    cv = cache_v.reshape(cache_v.shape[0], db, clen, nk)
    row = lambda bi: (bi, 0)
    cache = lambda bi: (layer, bi, 0, 0)
    return pl.pallas_call(
        functools.partial(_swa_attn_sample_kernel, t=t, past=past),
        grid=(db,),
        in_specs=[
            pl.BlockSpec(memory_space=pltpu.SMEM),
            pl.BlockSpec((t, D_MODEL), row),
            pl.BlockSpec((None, 1, clen, nk), cache), pl.BlockSpec((None, 1, clen, nk), cache),
            pl.BlockSpec((t, nk), row), pl.BlockSpec((t, nk), row),
        ],
        out_specs=pl.BlockSpec((t, D_MODEL), row),
        out_shape=jax.ShapeDtypeStruct((n, D_MODEL), BF16),
        compiler_params=_params(("parallel",)),
        name="swa_attn_sample",
    )(sinks, q, ck, cv, kb, vb)


def _post_kernel(x_ref, o_ref, wo_ref, gf_ref, wg_ref, wu_ref, cw_ref, cb_ref, wout_ref, st_ref,
                 p_ref, gp_ref, wpg_ref, wpp_ref, gfin_ref, y_ref, cs_ref, carry_sc, *, final):
    ti = pl.program_id(1)
    ns, tt, _ = x_ref.shape
    rows = ns * tt
    x = x_ref[...].reshape(rows, D_MODEL)
    x = x + _dot(o_ref[...].reshape(rows, o_ref.shape[-1]), wo_ref[...])
    h = _rms(x, gf_ref[...]).astype(BF16)
    t_idx = lax.broadcasted_iota(jnp.int32, (1, tt, 1), 1)

    @pl.when(ti == 0)
    def _():
        carry_sc[...] = st_ref[...]

    proj = _dot(p_ref[...].reshape(rows, PLE_DIM).astype(BF16), wpp_ref[...])
    acc = jnp.zeros((rows, D_MODEL), F32)
    nxt = (_dot(h, wg_ref[0]), _dot(h, wu_ref[0]))
    for c in range(N_FF_CHUNKS):
        sl = slice(c * FF_CHUNK, (c + 1) * FF_CHUNK)
        gate, up = nxt
        if c + 1 < N_FF_CHUNKS:
            nxt = (_dot(h, wg_ref[c + 1]), _dot(h, wu_ref[c + 1]))
        st = carry_sc[:, :, sl]
        g1 = pltpu.roll(gate, 1, 0).reshape(ns, tt, FF_CHUNK)
        g2 = pltpu.roll(gate, 2, 0).reshape(ns, tt, FF_CHUNK)
        gate = gate.reshape(ns, tt, FF_CHUNK)
        g1 = jnp.where(t_idx == 0, st[:, 1:2, :], g1)
        g2 = jnp.where(t_idx == 0, st[:, 0:1, :], jnp.where(t_idx == 1, st[:, 1:2, :], g2))
        cw = cw_ref[:, sl]
        conv = g2 * cw[0:1, :] + g1 * cw[1:2, :] + gate * cw[2:3, :] + cb_ref[:, sl]
        new_st = gate[:, tt - 2:, :]
        carry_sc[:, :, sl] = new_st
        cs_ref[:, :, sl] = new_st
        act = 0.5 * conv * (1.0 + lax.erf(conv * (0.5 ** 0.5)))
        y = (act * up.reshape(ns, tt, FF_CHUNK)).reshape(rows, FF_CHUNK)
        acc = acc + _dot(y.astype(BF16), wout_ref[c])
    x = x + acc
    hp = _rms(x, gp_ref[...]).astype(BF16)
    gate_p = jax.nn.sigmoid(_dot(hp, wpg_ref[...]))
    x = x + gate_p * proj
    if final:
        x = _rms(x, gfin_ref[...])
    y_ref[...] = x.reshape(ns, tt, D_MODEL)


def _post(x3, o2d, w, state, p4, layer, g_final, ns, tt, final):
    bx, tx, _ = x3.shape
    do = o2d.shape[-1]
    o3 = o2d.reshape(bx, tx, do)
    blk = lambda bi, ti: (bi, ti, 0)
    return pl.pallas_call(
        functools.partial(_post_kernel, final=final),
        grid=(bx // ns, tx // tt),
        in_specs=[
            pl.BlockSpec((ns, tt, D_MODEL), blk),
            pl.BlockSpec((ns, tt, do), blk),
            _const_spec(w["wo"].shape),
            _const_spec((1, D_MODEL)),
            _const_spec(w["wg"].shape),
            _const_spec(w["wu"].shape),
            _const_spec(w["cw"].shape),
            _const_spec(w["cb"].shape),
            _const_spec(w["wout"].shape),
            pl.BlockSpec((ns, 2, D_FF), lambda bi, ti: (bi, 0, 0)),
            pl.BlockSpec((None, ns, tt, PLE_DIM), lambda bi, ti: (layer, bi, ti, 0)),
            _const_spec((1, D_MODEL)),
            _const_spec(w["wpg"].shape),
            _const_spec(w["wpp"].shape),
            _const_spec((1, D_MODEL)),
        ],
        out_specs=[pl.BlockSpec((ns, tt, D_MODEL), blk),
                   pl.BlockSpec((ns, 2, D_FF), lambda bi, ti: (bi, 0, 0))],
        out_shape=[jax.ShapeDtypeStruct(x3.shape, F32), jax.ShapeDtypeStruct((bx, 2, D_FF), F32)],
        scratch_shapes=[pltpu.VMEM((ns, 2, D_FF), F32)],
        compiler_params=_params(("parallel", "arbitrary")),
        name="post_ffn",
    )(x3, o3, w["wo"], w["gf"], w["wg"], w["wu"], w["cw"], w["cb"], w["wout"], state, p4,
      w["gp"], w["wpg"], w["wpp"], g_final)


def _rot_cols(w):
    half = w.shape[-1] // 2
    return jnp.concatenate([-w[..., half:], w[..., :half]], axis=-1)


def _rope_tables(pos):
    half = MLA_ROPE // 2
    inv = ROPE_THETA ** (-jnp.arange(half, dtype=F32) / half)
    ang = pos.astype(F32)[:, None] * inv[None, :]
    reps = LANES // half
    return jnp.tile(jnp.cos(ang), (1, reps)), jnp.tile(jnp.sin(ang), (1, reps))


def _prep_mla(w_dq, g_q, w_uq, w_dkv, g_kv, w_uk, w_uv):
    z = jnp.zeros((D_MODEL, LANES - MLA_ROPE), F32)
    kr = w_dkv[:, MLA_KV_LORA:]
    wh = jnp.concatenate([w_dq, w_dkv[:, :MLA_KV_LORA], kr, z, _rot_cols(kr), z], axis=1)
    uq = w_uq.reshape(MLA_Q_LORA, MLA_HEADS, MLA_NOPE + MLA_ROPE)
    rp = uq[:, :, MLA_NOPE:]
    pad = jnp.zeros((MLA_Q_LORA, MLA_HEADS, LANES - MLA_ROPE), F32)
    nh = MLA_HEADS * LANES
    wuq = jnp.concatenate([uq[:, :, :MLA_NOPE].reshape(MLA_Q_LORA, nh),
                           jnp.concatenate([rp, pad], -1).reshape(MLA_Q_LORA, nh),
                           jnp.concatenate([_rot_cols(rp), pad], -1).reshape(MLA_Q_LORA, nh)], axis=1)
    wuq = wuq.astype(BF16)
    return dict(wh=wh.astype(BF16), gq=g_q[None, :], wuq=wuq, wuq_t=wuq.T, gkv=g_kv[None, :],
                wuk=jnp.transpose(w_uk, (1, 2, 0)).astype(BF16),
                wuv=jnp.transpose(w_uv, (1, 0, 2)).astype(BF16),
                wuk_flat=w_uk.reshape(MLA_KV_LORA, MLA_HEADS * MLA_NOPE).astype(BF16),
                wuv_flat_t=w_uv.reshape(MLA_KV_LORA, MLA_HEADS * MLA_V).T.astype(BF16))


def _prep_swa(w_qkv, b_qkv, sinks, w_o):
    nq = SWA_HEADS * SWA_HEAD_DIM
    nk = SWA_KV_HEADS * SWA_HEAD_DIM
    perm = jnp.array(SWA_PERM)

    def split(a):
        lead = a.shape[:-1]
        q = a[..., :nq].reshape(*lead, SWA_HEADS, SWA_HEAD_DIM)[..., perm, :]
        k = a[..., nq:nq + nk].reshape(*lead, SWA_KV_HEADS, SWA_HEAD_DIM)
        return jnp.concatenate([q.reshape(*lead, nq), _rot_cols(q).reshape(*lead, nq),
                                k.reshape(*lead, nk), _rot_cols(k).reshape(*lead, nk), a[..., nq + nk:]], axis=-1)

    wo = w_o.reshape(SWA_HEADS, SWA_HEAD_DIM, D_MODEL)[perm].reshape(nq, D_MODEL)
    return split(w_qkv).astype(BF16), split(b_qkv)[None, :], sinks[perm], wo.astype(BF16)


def _prep_post(w_o, g_ffn, w_in, conv_w, conv_b, w_out, g_ple, w_gate, w_proj):
    def chunks(w):
        return jnp.transpose(w.reshape(D_MODEL, N_FF_CHUNKS, FF_CHUNK), (1, 0, 2)).astype(BF16)
    return dict(wo=w_o.astype(BF16), gf=g_ffn[None, :], wg=chunks(w_in[:, :D_FF]), wu=chunks(w_in[:, D_FF:]),
                cw=conv_w, cb=conv_b[None, :], wout=w_out.reshape(N_FF_CHUNKS, FF_CHUNK, D_MODEL).astype(BF16),
                gp=g_ple[None, :], wpg=w_gate.astype(BF16), wpp=w_proj.astype(BF16))


def kernel(x_prompt, x_sample, p_prompt, p_sample, cache_mla_ckv, cache_mla_krope, cache_sb_k, cache_sb_v, cache_swa_k, cache_swa_v, state_ffn_conv, g_mix, g_ffn, g_ple, g_final, w_mla_dq, g_mla_q, w_mla_uq, w_mla_dkv, g_mla_kv, w_mla_uk, w_mla_uv, w_mla_o, w_sb_qkv, w_sb_o, w_swa_qkv, b_swa_qkv, swa_sinks, w_swa_o, w_ffn_in, ffn_conv_w, ffn_conv_b, w_ffn_out, w_ple_gate, w_ple_proj):
    b, s, _ = x_prompt.shape
    db, t, _ = x_sample.shape
    depth = g_mix.shape[0]
    past = cache_mla_ckv.shape[2]
    n_p = b * s
    n_s = db * t
    assert s % 512 == 0 and t == CHUNK and past % 256 == 0 and n_s % 512 == 0
    assert cache_swa_k.shape[2] == SWA_WINDOW and 512 % t == 0

    tm_mla = 256
    tm = 512
    tt = 512
    ns = 512 // t
    cos_p, sin_p = _rope_tables(jnp.arange(s, dtype=jnp.int32))
    cos_s, sin_s = _rope_tables(past + (jnp.arange(tm, dtype=jnp.int32) % t))
    zero_state = jnp.zeros((b, 2, D_FF), F32)
    g_fin = g_final[None, :]

    xp, xs = x_prompt, x_sample
    outs = {k: [] for k in ("ckv_p", "kr_p", "ckv_s", "kr_s", "sbk_p", "sbv_p", "sbk_s", "sbv_s",
                            "swk_p", "swv_p", "swk_s", "swv_s", "conv_p", "conv_s")}
    for i in range(depth):
        j = i // N_MIXERS
        gm = g_mix[i][None, :]
        xp2 = xp.reshape(n_p, D_MODEL)
        xs2 = xs.reshape(n_s, D_MODEL)
        if i % N_MIXERS == 0:
            w = _prep_mla(w_mla_dq[j], g_mla_q[j], w_mla_uq[j], w_mla_dkv[j], g_mla_kv[j], w_mla_uk[j], w_mla_uv[j])
            q_p, ckv_p, kr_p, k_p, v_p = _mla_project_t(xp2, gm, w, cos_p, sin_p, tm_mla)
            q_s, ckv_s, kr_s, kcat_s = _mla_project(xs2, gm, w, cos_s[:tm_mla], sin_s[:tm_mla], tm_mla)
            o_p = _mla_attention(q_p, k_p, v_p, b, s, 256)
            o_s = _mla_attention_sample(q_s, cache_mla_ckv, cache_mla_krope, kcat_s, w["wuv"], j)
            outs["ckv_p"].append(ckv_p.reshape(b, s, MLA_KV_LORA))
            outs["kr_p"].append(kr_p.reshape(b, s, MLA_ROPE))
            outs["ckv_s"].append(ckv_s.reshape(db, t, MLA_KV_LORA))
            outs["kr_s"].append(kr_s.reshape(db, t, MLA_ROPE))
            w_o = w_mla_o[j]
        elif i % N_MIXERS == 1:
            wq = w_sb_qkv[j].astype(BF16)
            q_t, kb_p, v_t, k_p, v_p = _sb_project_t(xp2, gm, wq, b, tm)
            qkv_s, k_s, v_s = _sb_project(xs2, gm, wq, tm)
            o_p = _sb_attention(q_t, kb_p, v_t, b, s, 256)
            o_s = _sb_attention_sample(qkv_s, cache_sb_k, cache_sb_v, j, db, 256)
            to_cache = lambda a: jnp.transpose(a.reshape(b, SB_HEADS, SB_HEAD_DIM, s), (0, 3, 1, 2))
            outs["sbk_p"].append(to_cache(k_p))
            outs["sbv_p"].append(to_cache(v_p))
            outs["sbk_s"].append(k_s.reshape(db, t, SB_HEADS, SB_HEAD_DIM))
            outs["sbv_s"].append(v_s.reshape(db, t, SB_HEADS, SB_HEAD_DIM))
            w_o = w_sb_o[j]
        else:
            wq, bq, sinks, w_o = _prep_swa(w_swa_qkv[j], b_swa_qkv[j], swa_sinks[j], w_swa_o[j])
            q_p, k_p, v_p, kb_p, vb_p = _swa_project(xp2, gm, wq, bq, cos_p, sin_p, tm)
            q_s, k_s, v_s, kb_s, vb_s = _swa_project(xs2, gm, wq, bq, cos_s, sin_s, tm)
            o_p = _swa_attention(sinks, q_p, kb_p, vb_p, b, s)
            o_s = _swa_attention_sample(sinks, q_s, cache_swa_k, cache_swa_v, kb_s, vb_s, j, db, past)
            kv_shape = (SWA_KV_HEADS, SWA_HEAD_DIM)
            outs["swk_p"].append(k_p.reshape(b, s, *kv_shape)[:, -SWA_WINDOW:])
            outs["swv_p"].append(v_p.reshape(b, s, *kv_shape)[:, -SWA_WINDOW:])
            outs["swk_s"].append(jnp.concatenate([cache_swa_k[j], k_s.reshape(db, t, *kv_shape)], 1)[:, -SWA_WINDOW:])
            outs["swv_s"].append(jnp.concatenate([cache_swa_v[j], v_s.reshape(db, t, *kv_shape)], 1)[:, -SWA_WINDOW:])
        wp = _prep_post(w_o, g_ffn[i], w_ffn_in[i], ffn_conv_w[i], ffn_conv_b[i], w_ffn_out[i],
                        g_ple[i], w_ple_gate[i], w_ple_proj[i])
        final = i == depth - 1
        xp, cp = _post(xp, o_p, wp, zero_state, p_prompt, i, g_fin, 1, tt, final)
        xs, cs = _post(xs, o_s, wp, state_ffn_conv[i], p_sample, i, g_fin, ns, t, final)
        outs["conv_p"].append(cp)
        outs["conv_s"].append(cs)
    order = ("ckv_p", "kr_p", "ckv_s", "kr_s", "sbk_p", "sbv_p", "sbk_s", "sbv_s",
             "swk_p", "swv_p", "swk_s", "swv_s", "conv_p", "conv_s")
    return (xp, xs) + tuple(jnp.stack(outs[k]) for k in order)
```

```python
import functools

import jax
import jax.numpy as jnp
from jax import lax
from jax.experimental import pallas as pl
from jax.experimental.pallas import tpu as pltpu

F32 = jnp.float32
BF16 = jnp.bfloat16

D_MODEL = 1024
CHUNK = 64
N_MIXERS = 3
RMS_EPS = 1e-6
ROPE_THETA = 10000.0
NEG = -1e30
LOG2E = 1.4426950408889634

MLA_HEADS = 16
MLA_NOPE = 128
MLA_ROPE = 64
MLA_V = 128
MLA_Q_LORA = 384
MLA_KV_LORA = 256
MLA_SCALE = (MLA_NOPE + MLA_ROPE) ** -0.5
MLA_QK = MLA_KV_LORA + 128
MLA_KDIM = MLA_NOPE + 128
MLA_HEAD_GROUP = 8

SB_HEADS = 16
SB_HEAD_DIM = 64
SB_SCALE = SB_HEAD_DIM ** -0.5

SWA_HEADS = 16
SWA_KV_HEADS = 4
SWA_GROUP = SWA_HEADS // SWA_KV_HEADS
SWA_HEAD_DIM = 64
SWA_WINDOW = 128
SWA_SCALE = SWA_HEAD_DIM ** -0.5
SWA_PERM = tuple(8 * p + 4 * s + g for p in range(2) for g in range(SWA_GROUP) for s in range(2))

D_FF = 2816
FF_CHUNK = 256
N_FF_CHUNKS = D_FF // FF_CHUNK
PLE_DIM = 256

LANES = 128
VMEM_LIMIT = 56 * 1024 * 1024


def _dot(a, b):
    return jnp.dot(a, b, preferred_element_type=F32)


def _dot_nt(a, b):
    return lax.dot_general(a, b, (((1,), (1,)), ((), ())), preferred_element_type=F32)


def _rms(x, g):
    ms = jnp.mean(x * x, axis=-1, keepdims=True)
    return x * lax.rsqrt(ms + RMS_EPS) * g


def _const_spec(shape):
    nd = len(shape)
    return pl.BlockSpec(shape, lambda *_: (0,) * nd, pipeline_mode=pl.Buffered(1))


def _params(sem):
    return pltpu.CompilerParams(dimension_semantics=sem, vmem_limit_bytes=VMEM_LIMIT)


def _mla_latents(x_ref, g_ref, wh_ref, gq_ref, gkv_ref, cos, sin, ckv_ref, kr_ref, kcat_ref):
    h = _rms(x_ref[...], g_ref[...]).astype(BF16)
    y = _dot(h, wh_ref[...])
    cq = _rms(y[:, :MLA_Q_LORA], gq_ref[...])
    o = MLA_Q_LORA
    ckv = _rms(y[:, o:o + MLA_KV_LORA], gkv_ref[...])
    o += MLA_KV_LORA
    kr = y[:, o:o + LANES] * cos + y[:, o + LANES:o + 2 * LANES] * sin
    ckv_ref[...] = ckv
    kr_ref[...] = kr[:, :MLA_ROPE]
    if kcat_ref is not None:
        kcat_ref[...] = jnp.concatenate([ckv, kr], axis=-1).astype(BF16)
    return cq, ckv, kr


def _mla_proj_t_kernel(x_ref, g_ref, wh_ref, gq_ref, wuq_ref, wuk_ref, wuv_ref, gkv_ref, cos_ref, sin_ref,
                       cos_t_ref, sin_t_ref, q_ref, ckv_ref, kr_ref, k_ref, v_ref):
    cq, ckv, kr = _mla_latents(x_ref, g_ref, wh_ref, gq_ref, gkv_ref, cos_ref[...], sin_ref[...],
                               ckv_ref, kr_ref, None)
    ckv_b = ckv.astype(BF16)
    kn = _dot(ckv_b, wuk_ref[...])
    kr_b = kr.astype(BF16)
    for hd in range(MLA_HEADS):
        k_ref[:, hd * MLA_KDIM:hd * MLA_KDIM + LANES] = kn[:, hd * LANES:(hd + 1) * LANES].astype(BF16)
        k_ref[:, hd * MLA_KDIM + LANES:(hd + 1) * MLA_KDIM] = kr_b
    v_ref[...] = _dot(wuv_ref[...], ckv.T.astype(BF16)).astype(BF16)
    qa = _dot(wuq_ref[...], cq.T.astype(BF16))
    cos = cos_t_ref[...]
    sin = sin_t_ref[...]
    nh = MLA_HEADS * LANES
    for hd in range(MLA_HEADS):
        sl = slice(hd * LANES, (hd + 1) * LANES)
        qr = qa[nh:2 * nh][sl] * cos + qa[2 * nh:][sl] * sin
        q_ref[hd] = (jnp.concatenate([qa[sl], qr], axis=0) * (MLA_SCALE * LOG2E)).astype(BF16)


def _mla_project_t(x2d, g, w, cos, sin, tm):
    n = x2d.shape[0]
    ntab = cos.shape[0] // tm
    row = lambda i: (i, 0)
    col = lambda i: (0, i)
    return pl.pallas_call(
        _mla_proj_t_kernel,
        grid=(n // tm,),
        in_specs=[
            pl.BlockSpec((tm, D_MODEL), row),
            _const_spec((1, D_MODEL)),
            _const_spec(w["wh"].shape),
            _const_spec((1, MLA_Q_LORA)),
            _const_spec(w["wuq_t"].shape),
            _const_spec(w["wuk_flat"].shape),
            _const_spec(w["wuv_flat_t"].shape),
            _const_spec((1, MLA_KV_LORA)),
            pl.BlockSpec((tm, LANES), lambda i: (i % ntab, 0)),
            pl.BlockSpec((tm, LANES), lambda i: (i % ntab, 0)),
            pl.BlockSpec((LANES, tm), lambda i: (0, i % ntab)),
            pl.BlockSpec((LANES, tm), lambda i: (0, i % ntab)),
        ],
        out_specs=[
            pl.BlockSpec((MLA_HEADS, MLA_KDIM, tm), lambda i: (0, 0, i)),
            pl.BlockSpec((tm, MLA_KV_LORA), row),
            pl.BlockSpec((tm, MLA_ROPE), row),
            pl.BlockSpec((tm, MLA_HEADS * MLA_KDIM), row),
            pl.BlockSpec((MLA_HEADS * MLA_V, tm), col),
        ],
        out_shape=[
            jax.ShapeDtypeStruct((MLA_HEADS, MLA_KDIM, n), BF16),
            jax.ShapeDtypeStruct((n, MLA_KV_LORA), F32),
            jax.ShapeDtypeStruct((n, MLA_ROPE), F32),
            jax.ShapeDtypeStruct((n, MLA_HEADS * MLA_KDIM), BF16),
            jax.ShapeDtypeStruct((MLA_HEADS * MLA_V, n), BF16),
        ],
        compiler_params=_params(("parallel",)),
        name="mla_proj_t",
    )(x2d, g, w["wh"], w["gq"], w["wuq_t"], w["wuk_flat"], w["wuv_flat_t"], w["gkv"], cos, sin, cos.T, sin.T)


def _mla_proj_kernel(x_ref, g_ref, wh_ref, gq_ref, wuq_ref, wuk_ref, gkv_ref, cos_ref, sin_ref,
                     q_ref, ckv_ref, kr_ref, kcat_ref):
    cos = cos_ref[...]
    sin = sin_ref[...]
    cq, _, _ = _mla_latents(x_ref, g_ref, wh_ref, gq_ref, gkv_ref, cos, sin, ckv_ref, kr_ref, kcat_ref)
    qa = _dot(cq.astype(BF16), wuq_ref[...])
    nh = MLA_HEADS * LANES
    for hd in range(MLA_HEADS):
        sl = slice(hd * LANES, (hd + 1) * LANES)
        ql = _dot(qa[:, sl].astype(BF16), wuk_ref[hd])
        qr = qa[:, nh:2 * nh][:, sl] * cos + qa[:, 2 * nh:][:, sl] * sin
        q_ref[hd] = (jnp.concatenate([ql, qr], axis=-1) * MLA_SCALE).astype(BF16)


def _mla_project(x2d, g, w, cos, sin, tm):
    n = x2d.shape[0]
    ntab = cos.shape[0] // tm
    row = lambda i: (i, 0)
    tab = lambda i: (i % ntab, 0)
    return pl.pallas_call(
        _mla_proj_kernel,
        grid=(n // tm,),
        in_specs=[
            pl.BlockSpec((tm, D_MODEL), row),
            _const_spec((1, D_MODEL)),
            _const_spec(w["wh"].shape),
            _const_spec((1, MLA_Q_LORA)),
            _const_spec(w["wuq"].shape),
            _const_spec(w["wuk"].shape),
            _const_spec((1, MLA_KV_LORA)),
            pl.BlockSpec((tm, LANES), tab),
            pl.BlockSpec((tm, LANES), tab),
        ],
        out_specs=[
            pl.BlockSpec((MLA_HEADS, tm, MLA_QK), lambda i: (0, i, 0)),
            pl.BlockSpec((tm, MLA_KV_LORA), row),
            pl.BlockSpec((tm, MLA_ROPE), row),
            pl.BlockSpec((tm, MLA_QK), row),
        ],
        out_shape=[
            jax.ShapeDtypeStruct((MLA_HEADS, n, MLA_QK), BF16),
            jax.ShapeDtypeStruct((n, MLA_KV_LORA), F32),
            jax.ShapeDtypeStruct((n, MLA_ROPE), F32),
            jax.ShapeDtypeStruct((n, MLA_QK), BF16),
        ],
        compiler_params=_params(("parallel",)),
        name="mla_proj",
    )(x2d, g, w["wh"], w["gq"], w["wuq"], w["wuk"], w["gkv"], cos, sin)


def _sb_proj_kernel(x_ref, g_ref, w_ref, qkv_ref, k_ref, v_ref):
    h = _rms(x_ref[...], g_ref[...]).astype(BF16)
    y = _dot(h, w_ref[...])
    k_ref[...] = y[:, D_MODEL:2 * D_MODEL]
    v_ref[...] = y[:, 2 * D_MODEL:]
    qkv_ref[:, :D_MODEL] = (y[:, :D_MODEL] * (SB_SCALE * LOG2E)).astype(BF16)
    qkv_ref[:, D_MODEL:] = y[:, D_MODEL:].astype(BF16)


def _sb_proj_t_kernel(x_ref, g_ref, w_ref, q_t_ref, kb_ref, v_t_ref, k_t_ref, vf_t_ref):
    h = _rms(x_ref[...], g_ref[...]).astype(BF16)
    y = _dot(h, w_ref[...])
    k = y[:, D_MODEL:2 * D_MODEL]
    v_t = y[:, 2 * D_MODEL:].T
    k_t_ref[0] = k.T
    vf_t_ref[0] = v_t
    kb_ref[...] = k.astype(BF16)
    q_t_ref[...] = (y[:, :D_MODEL] * (SB_SCALE * LOG2E)).T.astype(BF16)
    v_t_ref[...] = v_t.astype(BF16)


def _sb_project_t(x2d, g, w, b, tm):
    n = x2d.shape[0]
    nt = n // b // tm
    row = lambda i: (i, 0)
    col = lambda i: (0, i)
    per_batch = lambda i: (i // nt, 0, i % nt)
    return pl.pallas_call(
        _sb_proj_t_kernel,
        grid=(n // tm,),
        in_specs=[pl.BlockSpec((tm, D_MODEL), row), _const_spec((1, D_MODEL)), _const_spec(w.shape)],
        out_specs=[pl.BlockSpec((D_MODEL, tm), col), pl.BlockSpec((tm, D_MODEL), row),
                   pl.BlockSpec((D_MODEL, tm), col), pl.BlockSpec((1, D_MODEL, tm), per_batch),
                   pl.BlockSpec((1, D_MODEL, tm), per_batch)],
        out_shape=[jax.ShapeDtypeStruct((D_MODEL, n), BF16), jax.ShapeDtypeStruct((n, D_MODEL), BF16),
                   jax.ShapeDtypeStruct((D_MODEL, n), BF16), jax.ShapeDtypeStruct((b, D_MODEL, n // b), F32),
                   jax.ShapeDtypeStruct((b, D_MODEL, n // b), F32)],
        compiler_params=_params(("parallel",)),
        name="sb_proj_t",
    )(x2d, g, w)


def _sb_project(x2d, g, w, tm):
    n = x2d.shape[0]
    row = lambda i: (i, 0)
    return pl.pallas_call(
        _sb_proj_kernel,
        grid=(n // tm,),
        in_specs=[pl.BlockSpec((tm, D_MODEL), row), _const_spec((1, D_MODEL)), _const_spec(w.shape)],
        out_specs=[pl.BlockSpec((tm, 3 * D_MODEL), row), pl.BlockSpec((tm, D_MODEL), row),
                   pl.BlockSpec((tm, D_MODEL), row)],
        out_shape=[jax.ShapeDtypeStruct((n, 3 * D_MODEL), BF16), jax.ShapeDtypeStruct((n, D_MODEL), F32),
                   jax.ShapeDtypeStruct((n, D_MODEL), F32)],
        compiler_params=_params(("parallel",)),
        name="sb_proj",
    )(x2d, g, w)


def _swa_proj_kernel(x_ref, g_ref, w_ref, b_ref, cos_ref, sin_ref, q_ref, k_ref, v_ref, kb_ref, vb_ref):
    h = _rms(x_ref[...], g_ref[...]).astype(BF16)
    y = _dot(h, w_ref[...]) + b_ref[...]
    cos = cos_ref[...]
    sin = sin_ref[...]
    nq = SWA_HEADS * SWA_HEAD_DIM
    nk = SWA_KV_HEADS * SWA_HEAD_DIM
    for c in range(nq // LANES):
        sl = slice(c * LANES, (c + 1) * LANES)
        q = y[:, :nq][:, sl] * cos + y[:, nq:2 * nq][:, sl] * sin
        q_ref[:, sl] = (q * SWA_SCALE).astype(BF16)
    o = 2 * nq
    for c in range(nk // LANES):
        sl = slice(c * LANES, (c + 1) * LANES)
        k = y[:, o:o + nk][:, sl] * cos + y[:, o + nk:o + 2 * nk][:, sl] * sin
        k_ref[:, sl] = k
        kb_ref[:, sl] = k.astype(BF16)
    v = y[:, o + 2 * nk:]
    v_ref[...] = v
    vb_ref[...] = v.astype(BF16)


def _swa_project(x2d, g, w, b, cos, sin, tm):
    n = x2d.shape[0]
    ntab = cos.shape[0] // tm
    nq = SWA_HEADS * SWA_HEAD_DIM
    nk = SWA_KV_HEADS * SWA_HEAD_DIM
    row = lambda i: (i, 0)
    tab = lambda i: (i % ntab, 0)
    return pl.pallas_call(
        _swa_proj_kernel,
        grid=(n // tm,),
        in_specs=[pl.BlockSpec((tm, D_MODEL), row), _const_spec((1, D_MODEL)), _const_spec(w.shape),
                  _const_spec(b.shape), pl.BlockSpec((tm, LANES), tab), pl.BlockSpec((tm, LANES), tab)],
        out_specs=[pl.BlockSpec((tm, nq), row), pl.BlockSpec((tm, nk), row), pl.BlockSpec((tm, nk), row),
                   pl.BlockSpec((tm, nk), row), pl.BlockSpec((tm, nk), row)],
        out_shape=[jax.ShapeDtypeStruct((n, nq), BF16), jax.ShapeDtypeStruct((n, nk), F32),
                   jax.ShapeDtypeStruct((n, nk), F32), jax.ShapeDtypeStruct((n, nk), BF16),
                   jax.ShapeDtypeStruct((n, nk), BF16)],
        compiler_params=_params(("parallel",)),
        name="swa_proj",
    )(x2d, g, w, b, cos, sin)


def _mla_attn_kernel(q_ref, k_ref, v_ref, o_ref, m_sc, l_sc, acc_sc, *, t):
    qi = pl.program_id(2)
    nh = q_ref.shape[0]
    m_sc[...] = jnp.full(m_sc.shape, NEG, F32)
    l_sc[...] = jnp.zeros(l_sc.shape, F32)
    acc_sc[...] = jnp.zeros(acc_sc.shape, F32)
    k_chunk = lax.broadcasted_iota(jnp.int32, (t, 1), 0) // CHUNK
    q_chunk = lax.broadcasted_iota(jnp.int32, (1, t), 1) // CHUNK
    diag_mask = k_chunk <= q_chunk

    def steps(blocks):
        work = [(j, mask, hd) for j, mask in blocks for hd in range(nh)]

        def scores(i):
            j, _, hd = work[i]
            return _dot(k_ref[0, pl.ds(pl.multiple_of(j * t, t), t), hd * MLA_KDIM:(hd + 1) * MLA_KDIM], q_ref[hd])

        ahead = 3
        pending = [scores(i) for i in range(ahead)]
        for i, (j, mask, hd) in enumerate(work):
            start = pl.multiple_of(j * t, t)
            s = pending.pop(0)
            if i + ahead < len(work):
                pending.append(scores(i + ahead))
            if mask is not None:
                s = jnp.where(mask, s, NEG)
            m_old = m_sc[hd]
            m_new = jnp.maximum(m_old, jnp.max(s, axis=0, keepdims=True))
            alpha = jnp.exp2(m_old - m_new)
            p = jnp.exp2(s - m_new)
            l_sc[hd] = alpha * l_sc[hd] + jnp.sum(p, axis=0, keepdims=True)
            v = v_ref[hd * MLA_V:(hd + 1) * MLA_V, pl.ds(start, t)]
            acc_sc[hd] = alpha * acc_sc[hd] + _dot(v, p.astype(BF16))
            m_sc[hd] = m_new

    def body(jj, _):
        steps([(2 * jj, None), (2 * jj + 1, None)])
        return 0

    lax.fori_loop(0, qi // 2, body, 0)

    @pl.when(qi % 2 == 1)
    def _():
        steps([(qi - 1, None), (qi, diag_mask)])

    @pl.when(qi % 2 == 0)
    def _():
        steps([(qi, diag_mask)])
    for hd in range(nh):
        o_ref[:, hd * MLA_V:(hd + 1) * MLA_V] = (acc_sc[hd] / l_sc[hd]).T.astype(BF16)


def _mla_attention(q_t, k_full, v_t, b, s, t):
    n = b * s
    nq = s // t
    g = MLA_HEAD_GROUP
    return pl.pallas_call(
        functools.partial(_mla_attn_kernel, t=t),
        grid=(b, MLA_HEADS // g, nq),
        in_specs=[
            pl.BlockSpec((g, MLA_KDIM, t), lambda bi, hg, qi: (hg, 0, bi * nq + qi)),
            pl.BlockSpec((1, s, g * MLA_KDIM), lambda bi, hg, qi: (bi, 0, hg)),
            pl.BlockSpec((g * MLA_V, s), lambda bi, hg, qi: (hg, bi)),
        ],
        out_specs=pl.BlockSpec((t, g * MLA_V), lambda bi, hg, qi: (bi * nq + qi, hg)),
        out_shape=jax.ShapeDtypeStruct((n, MLA_HEADS * MLA_V), BF16),
        scratch_shapes=[pltpu.VMEM((g, 1, t), F32), pltpu.VMEM((g, 1, t), F32),
                        pltpu.VMEM((g, MLA_V, t), F32)],
        compiler_params=_params(("parallel", "parallel", "arbitrary")),
        name="mla_attn",
    )(q_t, k_full.reshape(b, s, MLA_HEADS * MLA_KDIM), v_t)


def _mla_attn_sample_kernel(q_ref, cc_ref, cr_ref, kn_ref, wuv_ref, o_ref, *, t, past):
    rows = MLA_HEADS * t
    q = q_ref[...].reshape(rows, MLA_QK)
    ck = cc_ref[0].astype(BF16)
    s_old = (_dot_nt(q[:, :MLA_KV_LORA], ck)
             + _dot(q[:, MLA_KV_LORA:MLA_KV_LORA + MLA_ROPE], cr_ref[0].astype(BF16)))
    kn = kn_ref[...]
    s_new = _dot_nt(q, kn)
    tok = lax.broadcasted_iota(jnp.int32, (rows, 1), 0) & (t - 1)
    k_chunk = (past + lax.broadcasted_iota(jnp.int32, (1, t), 1)) // CHUNK
    s_new = jnp.where(k_chunk <= (past + tok) // CHUNK, s_new, NEG)
    m = jnp.maximum(jnp.max(s_old, axis=-1, keepdims=True), jnp.max(s_new, axis=-1, keepdims=True))
    p_old = jnp.exp(s_old - m)
    p_new = jnp.exp(s_new - m)
    l = jnp.sum(p_old, axis=-1, keepdims=True) + jnp.sum(p_new, axis=-1, keepdims=True)
    acc = _dot(p_old.astype(BF16), ck) + _dot(p_new.astype(BF16), kn[:, :MLA_KV_LORA])
    o_lat = (acc / l).astype(BF16)
    for hd in range(MLA_HEADS):
        o = _dot(o_lat[hd * t:(hd + 1) * t], wuv_ref[hd])
        o_ref[:, hd * MLA_V:(hd + 1) * MLA_V] = o.astype(BF16)


def _mla_attention_sample(q, cache_ckv, cache_kr, kcat, wuv, layer):
    _, db, past, _ = cache_ckv.shape
    t = q.shape[1] // db
    return pl.pallas_call(
        functools.partial(_mla_attn_sample_kernel, t=t, past=past),
        grid=(db,),
        in_specs=[
            pl.BlockSpec((MLA_HEADS, t, MLA_QK), lambda bi: (0, bi, 0)),
            pl.BlockSpec((None, 1, past, MLA_KV_LORA), lambda bi: (layer, bi, 0, 0)),
            pl.BlockSpec((None, 1, MLA_ROPE, past), lambda bi: (layer, bi, 0, 0)),
            pl.BlockSpec((t, MLA_QK), lambda bi: (bi, 0)),
            _const_spec(wuv.shape),
        ],
        out_specs=pl.BlockSpec((t, MLA_HEADS * MLA_V), lambda bi: (bi, 0)),
        out_shape=jax.ShapeDtypeStruct((db * t, MLA_HEADS * MLA_V), BF16),
        compiler_params=_params(("parallel",)),
        name="mla_attn_sample",
    )(q, cache_ckv, jnp.swapaxes(cache_kr, 2, 3), kcat, wuv)


def _later_key_matrix(tk):
    r = lax.broadcasted_iota(jnp.int32, (tk, tk), 0)
    c = lax.broadcasted_iota(jnp.int32, (tk, tk), 1)
    return jnp.where(r > c, 1.0, 0.0).astype(BF16)


def _split_pair(q2):
    lane = lax.broadcasted_iota(jnp.int32, (1, LANES), 1)
    zero = jnp.zeros_like(q2)
    return jnp.where(lane < LANES // 2, q2, zero), jnp.where(lane >= LANES // 2, q2, zero), lane


def _sb_logits_stage(z, mask):
    log_sig = jnp.minimum(z, 0.0) - jnp.log(1.0 + jnp.exp2(jnp.minimum(z, -z))) * LOG2E
    log_not = log_sig - z
    if mask is not None:
        log_not = jnp.where(mask, log_not, 0.0)
    hi = log_not.astype(BF16)
    lo = (log_not - hi.astype(F32)).astype(BF16)
    return log_sig, log_not, hi, lo


def _sb_attn_kernel(q_ref, k_ref, v_ref, o_ref, carry_sc, acc_sc, *, t):
    qi = pl.program_id(1)
    low = lax.broadcasted_iota(jnp.int32, (LANES, 1), 0) < LANES // 2
    r = lax.broadcasted_iota(jnp.int32, (t, t), 0)
    c = lax.broadcasted_iota(jnp.int32, (t, t), 1)
    later = jnp.where(c > r, 1.0, 0.0).astype(BF16)
    later2 = jnp.concatenate([later, later], axis=1)
    diag_mask = r < c
    carry_sc[...] = jnp.zeros(carry_sc.shape, F32)
    acc_sc[...] = jnp.zeros(acc_sc.shape, F32)

    def pick(x, hd):
        zero = jnp.zeros_like(x)
        return jnp.where(low, x, zero) if hd % 2 == 0 else jnp.where(low, zero, x)

    def blocks(items):
        work = [(j, mask, hd) for j, mask in items for hd in range(SB_HEADS)]

        def logits(i):
            j, _, hd = work[i]
            g = hd // 2
            qh = pick(q_ref[g * LANES:(g + 1) * LANES, :], hd)
            start = pl.multiple_of(j * t, t)
            return _dot(k_ref[0, pl.ds(start, t), g * LANES:(g + 1) * LANES], qh)

        def stage1(i, z):
            log_sig, log_not, hi, lo = _sb_logits_stage(z, work[i][1])
            tail = _dot(later2, jnp.concatenate([hi, lo], axis=0))
            return i, log_sig, tail, jnp.sum(log_not, axis=0, keepdims=True)

        def stage2(i, log_sig, tail, colsum):
            j, mask, hd = work[i]
            carry = carry_sc[hd]
            a = jnp.exp2(log_sig + (tail + carry))
            if mask is not None:
                a = jnp.where(mask, a, 0.0)
            carry_sc[hd] = carry + colsum
            g = hd // 2
            vh = pick(v_ref[g * LANES:(g + 1) * LANES, pl.ds(pl.multiple_of(j * t, t), t)], hd)
            acc_sc[g] += _dot(vh, a.astype(BF16))

        ahead = 2
        zs = [logits(i) for i in range(ahead)]
        pending = None
        for i in range(len(work)):
            st = stage1(i, zs.pop(0))
            if i + ahead < len(work):
                zs.append(logits(i + ahead))
            if pending is not None:
                stage2(*pending)
            pending = st
        stage2(*pending)

    @pl.when(qi % 2 == 1)
    def _():
        blocks([(qi, diag_mask), (qi - 1, None)])

    @pl.when(qi % 2 == 0)
    def _():
        blocks([(qi, diag_mask)])

    rest = qi - qi % 2

    def body(jj, _):
        blocks([(rest - 1 - 2 * jj, None), (rest - 2 - 2 * jj, None)])
        return 0

    lax.fori_loop(0, rest // 2, body, 0)
    for g in range(SB_HEADS // 2):
        o_ref[:, g * LANES:(g + 1) * LANES] = acc_sc[g].T.astype(BF16)


def _sb_attention(q_t, kb, v_t, b, s, t):
    n = b * s
    nq = s // t
    return pl.pallas_call(
        functools.partial(_sb_attn_kernel, t=t),
        grid=(b, nq),
        in_specs=[
            pl.BlockSpec((D_MODEL, t), lambda bi, qi: (0, bi * nq + qi)),
            pl.BlockSpec((1, s, D_MODEL), lambda bi, qi: (bi, 0, 0)),
            pl.BlockSpec((D_MODEL, s), lambda bi, qi: (0, bi)),
        ],
        out_specs=pl.BlockSpec((t, D_MODEL), lambda bi, qi: (bi * nq + qi, 0)),
        out_shape=jax.ShapeDtypeStruct((n, D_MODEL), BF16),
        scratch_shapes=[pltpu.VMEM((SB_HEADS, 1, t), F32), pltpu.VMEM((SB_HEADS // 2, LANES, t), F32)],
        compiler_params=_params(("parallel", "arbitrary")),
        name="sb_attn",
    )(q_t, kb.reshape(b, s, D_MODEL), v_t)


def _sb_attn_sample_kernel(q_ref, kn_ref, vn_ref, ck_ref, cv_ref, o_ref, carry_sc, acc_sc, *, t, tk):
    j = pl.program_id(1)
    hdim = SB_HEAD_DIM

    def head(ref, hd):
        return ref[:, hd * hdim:(hd + 1) * hdim]

    def run(logits, weighted, later2, mask, first):
        def stage1(z):
            log_sig, log_not, hi, lo = _sb_logits_stage(z, mask)
            tail = _dot(jnp.concatenate([hi, lo], axis=1), later2)
            return log_sig, tail, jnp.sum(log_not, axis=-1, keepdims=True)

        def stage2(hd, log_sig, tail, rowsum):
            if first:
                a = jnp.exp2(log_sig + tail)
                carry_sc[hd] = rowsum
            else:
                carry = carry_sc[hd]
                a = jnp.exp2(log_sig + (tail + carry))
                carry_sc[hd] = carry + rowsum
            if mask is not None:
                a = jnp.where(mask, a, 0.0)
            o = weighted(hd, a.astype(BF16))
            acc_sc[hd] = o if first else acc_sc[hd] + o

        ahead, lag = 4, 2
        zs = [logits(hd) for hd in range(ahead)]
        pending = []
        for hd in range(SB_HEADS):
            pending.append((hd,) + stage1(zs.pop(0)))
            if hd + ahead < SB_HEADS:
                zs.append(logits(hd + ahead))
            if len(pending) > lag:
                stage2(*pending.pop(0))
        for item in pending:
            stage2(*item)

    def later_pair(n):
        u = _later_key_matrix(n)
        return jnp.concatenate([u, u], axis=0)

    @pl.when(j == 0)
    def _():
        r = lax.broadcasted_iota(jnp.int32, (t, t), 0)
        c = lax.broadcasted_iota(jnp.int32, (t, t), 1)
        run(lambda hd: _dot_nt(head(q_ref, hd), head(kn_ref, hd)),
            lambda hd, a: _dot(a, head(vn_ref, hd)), later_pair(t), c < r, True)

    run(lambda hd: _dot(head(q_ref, hd), ck_ref[0, hd].astype(BF16)),
        lambda hd, a: _dot_nt(a, cv_ref[0, hd].astype(BF16)), later_pair(tk), None, False)

    @pl.when(j == pl.num_programs(1) - 1)
    def _():
        o_ref[...] = jnp.concatenate([acc_sc[hd] for hd in range(SB_HEADS)], axis=-1).astype(BF16)


def _sb_attention_sample(qkv, cache_k, cache_v, layer, db, tk):
    n = qkv.shape[0]
    t = n // db
    past = cache_k.shape[2]
    nblk = past // tk
    cache = lambda bi, j: (layer, bi, 0, 0, nblk - 1 - j)
    cache_k = jnp.transpose(cache_k, (0, 1, 3, 4, 2))
    cache_v = jnp.transpose(cache_v, (0, 1, 3, 4, 2))
    return pl.pallas_call(
        functools.partial(_sb_attn_sample_kernel, t=t, tk=tk),
        grid=(db, nblk),
        in_specs=[
            pl.BlockSpec((t, D_MODEL), lambda bi, j: (bi, 0)),
            pl.BlockSpec((t, D_MODEL), lambda bi, j: (bi, 1)),
            pl.BlockSpec((t, D_MODEL), lambda bi, j: (bi, 2)),
            pl.BlockSpec((None, 1, SB_HEADS, SB_HEAD_DIM, tk), cache),
            pl.BlockSpec((None, 1, SB_HEADS, SB_HEAD_DIM, tk), cache),
        ],
        out_specs=pl.BlockSpec((t, D_MODEL), lambda bi, j: (bi, 0)),
        out_shape=jax.ShapeDtypeStruct((n, D_MODEL), BF16),
        scratch_shapes=[pltpu.VMEM((SB_HEADS, t, 1), F32), pltpu.VMEM((SB_HEADS, t, SB_HEAD_DIM), F32)],
        compiler_params=_params(("parallel", "arbitrary")),
        name="sb_attn_sample",
    )(qkv, qkv, qkv, cache_k, cache_v)


def _swa_heads(sink_ref, q_ref, kband, vband, mask, o_ref):
    lane = lax.broadcasted_iota(jnp.int32, (1, LANES), 1)

    def scores(hd):
        blk, half = divmod(hd, 2)
        pair = blk // SWA_GROUP
        qh = _split_pair(q_ref[:, blk * LANES:(blk + 1) * LANES])[half]
        return _dot_nt(qh, kband[:, pair * LANES:(pair + 1) * LANES])

    ahead = 3
    pending = [scores(hd) for hd in range(ahead)]
    outs = []
    for hd in range(SWA_HEADS):
        blk, half = divmod(hd, 2)
        pair = blk // SWA_GROUP
        s = jnp.where(mask, pending.pop(0), NEG)
        if hd + ahead < SWA_HEADS:
            pending.append(scores(hd + ahead))
        sink = sink_ref[hd]
        m = jnp.maximum(jnp.max(s, axis=-1, keepdims=True), sink)
        e = jnp.exp(s - m)
        den = jnp.sum(e, axis=-1, keepdims=True) + jnp.exp(sink - m)
        outs.append(_dot((e / den).astype(BF16), vband[:, pair * LANES:(pair + 1) * LANES]))
        if half:
            o_ref[:, blk * LANES:(blk + 1) * LANES] = jnp.where(lane < LANES // 2, outs[-2], outs[-1]).astype(BF16)


def _swa_attn_kernel(sink_ref, q_ref, kp_ref, kc_ref, vp_ref, vc_ref, o_ref, *, tq):
    i = pl.program_id(1)
    kband = jnp.concatenate([kp_ref[0], kc_ref[0]], axis=0)
    vband = jnp.concatenate([vp_ref[0], vc_ref[0]], axis=0)
    q_pos = i * tq + lax.broadcasted_iota(jnp.int32, (tq, 1), 0)
    k_pos = i * tq - SWA_WINDOW + lax.broadcasted_iota(jnp.int32, (1, SWA_WINDOW + tq), 1)
    dc = (q_pos + SWA_WINDOW) // CHUNK - (k_pos + SWA_WINDOW) // CHUNK
    mask = (dc >= 0) & (dc <= SWA_WINDOW // CHUNK) & (k_pos >= 0)
    _swa_heads(sink_ref, q_ref, kband, vband, mask, o_ref)


def _swa_attention(sinks, q, kb, vb, b, s):
    tq = SWA_WINDOW
    nq = s // tq
    nk = SWA_KV_HEADS * SWA_HEAD_DIM
    kb3 = kb.reshape(b, s, nk)
    vb3 = vb.reshape(b, s, nk)
    prev = lambda bi, qi: (bi, jnp.maximum(qi - 1, 0), 0)
    cur = lambda bi, qi: (bi, qi, 0)
    return pl.pallas_call(
        functools.partial(_swa_attn_kernel, tq=tq),
        grid=(b, nq),
        in_specs=[
            pl.BlockSpec(memory_space=pltpu.SMEM),
            pl.BlockSpec((tq, D_MODEL), lambda bi, qi: (bi * nq + qi, 0)),
            pl.BlockSpec((1, tq, nk), prev), pl.BlockSpec((1, tq, nk), cur),
            pl.BlockSpec((1, tq, nk), prev), pl.BlockSpec((1, tq, nk), cur),
        ],
        out_specs=pl.BlockSpec((tq, D_MODEL), lambda bi, qi: (bi * nq + qi, 0)),
        out_shape=jax.ShapeDtypeStruct((b * s, D_MODEL), BF16),
        compiler_params=_params(("parallel", "arbitrary")),
        name="swa_attn",
    )(sinks, q, kb3, kb3, vb3, vb3)


def _swa_attn_sample_kernel(sink_ref, q_ref, ck_ref, cv_ref, kn_ref, vn_ref, o_ref, *, t, past):
    clen = ck_ref.shape[1]
    kband = jnp.concatenate([ck_ref[0].astype(BF16), kn_ref[...]], axis=0)
    vband = jnp.concatenate([cv_ref[0].astype(BF16), vn_ref[...]], axis=0)
    q_pos = past + lax.broadcasted_iota(jnp.int32, (t, 1), 0)
    k_pos = past - clen + lax.broadcasted_iota(jnp.int32, (1, clen + t), 1)
    dc = q_pos // CHUNK - (k_pos + SWA_WINDOW) // CHUNK + SWA_WINDOW // CHUNK
    mask = (dc >= 0) & (dc <= SWA_WINDOW // CHUNK) & (k_pos >= 0)
    _swa_heads(sink_ref, q_ref, kband, vband, mask, o_ref)


def _swa_attention_sample(sinks, q, cache_k, cache_v, kb, vb, layer, db, past):
    n = q.shape[0]
    t = n // db
    clen = cache_k.shape[2]
    nk = SWA_KV_HEADS * SWA_HEAD_DIM
    ck = cache_k.reshape(cache_k.shape[0], db, clen, nk)
    cv = cache_v.reshape(cache_v.shape[0], db, clen, nk)
    row = lambda bi: (bi, 0)
    cache = lambda bi: (layer, bi, 0, 0)
    return pl.pallas_call(
        functools.partial(_swa_attn_sample_kernel, t=t, past=past),
        grid=(db,),
        in_specs=[
            pl.BlockSpec(memory_space=pltpu.SMEM),
            pl.BlockSpec((t, D_MODEL), row),
            pl.BlockSpec((None, 1, clen, nk), cache), pl.BlockSpec((None, 1, clen, nk), cache),
            pl.BlockSpec((t, nk), row), pl.BlockSpec((t, nk), row),
        ],
        out_specs=pl.BlockSpec((t, D_MODEL), row),
        out_shape=jax.ShapeDtypeStruct((n, D_MODEL), BF16),
        compiler_params=_params(("parallel",)),
        name="swa_attn_sample",
    )(sinks, q, ck, cv, kb, vb)


def _post_kernel(x_ref, o_ref, wo_ref, gf_ref, wg_ref, wu_ref, cw_ref, cb_ref, wout_ref, st_ref,
                 p_ref, gp_ref, wpg_ref, wpp_ref, gfin_ref, y_ref, cs_ref, carry_sc, *, final):
    ti = pl.program_id(1)
    ns, tt, _ = x_ref.shape
    rows = ns * tt
    x = x_ref[...].reshape(rows, D_MODEL)
    x = x + _dot(o_ref[...].reshape(rows, o_ref.shape[-1]), wo_ref[...])
    h = _rms(x, gf_ref[...]).astype(BF16)
    t_idx = lax.broadcasted_iota(jnp.int32, (1, tt, 1), 1)

    @pl.when(ti == 0)
    def _():
        carry_sc[...] = st_ref[...]

    proj = _dot(p_ref[...].reshape(rows, PLE_DIM).astype(BF16), wpp_ref[...])
    acc = jnp.zeros((rows, D_MODEL), F32)
    nxt = (_dot(h, wg_ref[0]), _dot(h, wu_ref[0]))
    for c in range(N_FF_CHUNKS):
        sl = slice(c * FF_CHUNK, (c + 1) * FF_CHUNK)
        gate, up = nxt
        if c + 1 < N_FF_CHUNKS:
            nxt = (_dot(h, wg_ref[c + 1]), _dot(h, wu_ref[c + 1]))
        st = carry_sc[:, :, sl]
        g1 = pltpu.roll(gate, 1, 0).reshape(ns, tt, FF_CHUNK)
        g2 = pltpu.roll(gate, 2, 0).reshape(ns, tt, FF_CHUNK)
        gate = gate.reshape(ns, tt, FF_CHUNK)
        g1 = jnp.where(t_idx == 0, st[:, 1:2, :], g1)
        g2 = jnp.where(t_idx == 0, st[:, 0:1, :], jnp.where(t_idx == 1, st[:, 1:2, :], g2))
        cw = cw_ref[:, sl]
        conv = g2 * cw[0:1, :] + g1 * cw[1:2, :] + gate * cw[2:3, :] + cb_ref[:, sl]
        new_st = gate[:, tt - 2:, :]
        carry_sc[:, :, sl] = new_st
        cs_ref[:, :, sl] = new_st
        act = 0.5 * conv * (1.0 + lax.erf(conv * (0.5 ** 0.5)))
        y = (act * up.reshape(ns, tt, FF_CHUNK)).reshape(rows, FF_CHUNK)
        acc = acc + _dot(y.astype(BF16), wout_ref[c])
    x = x + acc
    hp = _rms(x, gp_ref[...]).astype(BF16)
    gate_p = jax.nn.sigmoid(_dot(hp, wpg_ref[...]))
    x = x + gate_p * proj
    if final:
        x = _rms(x, gfin_ref[...])
    y_ref[...] = x.reshape(ns, tt, D_MODEL)


def _post(x3, o2d, w, state, p4, layer, g_final, ns, tt, final):
    bx, tx, _ = x3.shape
    do = o2d.shape[-1]
    o3 = o2d.reshape(bx, tx, do)
    blk = lambda bi, ti: (bi, ti, 0)
    return pl.pallas_call(
        functools.partial(_post_kernel, final=final),
        grid=(bx // ns, tx // tt),
        in_specs=[
            pl.BlockSpec((ns, tt, D_MODEL), blk),
            pl.BlockSpec((ns, tt, do), blk),
            _const_spec(w["wo"].shape),
            _const_spec((1, D_MODEL)),
            _const_spec(w["wg"].shape),
            _const_spec(w["wu"].shape),
            _const_spec(w["cw"].shape),
            _const_spec(w["cb"].shape),
            _const_spec(w["wout"].shape),
            pl.BlockSpec((ns, 2, D_FF), lambda bi, ti: (bi, 0, 0)),
            pl.BlockSpec((None, ns, tt, PLE_DIM), lambda bi, ti: (layer, bi, ti, 0)),
            _const_spec((1, D_MODEL)),
            _const_spec(w["wpg"].shape),
            _const_spec(w["wpp"].shape),
            _const_spec((1, D_MODEL)),
        ],
        out_specs=[pl.BlockSpec((ns, tt, D_MODEL), blk),
                   pl.BlockSpec((ns, 2, D_FF), lambda bi, ti: (bi, 0, 0))],
        out_shape=[jax.ShapeDtypeStruct(x3.shape, F32), jax.ShapeDtypeStruct((bx, 2, D_FF), F32)],
        scratch_shapes=[pltpu.VMEM((ns, 2, D_FF), F32)],
        compiler_params=_params(("parallel", "arbitrary")),
        name="post_ffn",
    )(x3, o3, w["wo"], w["gf"], w["wg"], w["wu"], w["cw"], w["cb"], w["wout"], state, p4,
      w["gp"], w["wpg"], w["wpp"], g_final)


def _rot_cols(w):
    half = w.shape[-1] // 2
    return jnp.concatenate([-w[..., half:], w[..., :half]], axis=-1)


def _rope_tables(pos):
    half = MLA_ROPE // 2
    inv = ROPE_THETA ** (-jnp.arange(half, dtype=F32) / half)
    ang = pos.astype(F32)[:, None] * inv[None, :]
    reps = LANES // half
    return jnp.tile(jnp.cos(ang), (1, reps)), jnp.tile(jnp.sin(ang), (1, reps))


def _prep_mla(w_dq, g_q, w_uq, w_dkv, g_kv, w_uk, w_uv):
    z = jnp.zeros((D_MODEL, LANES - MLA_ROPE), F32)
    kr = w_dkv[:, MLA_KV_LORA:]
    wh = jnp.concatenate([w_dq, w_dkv[:, :MLA_KV_LORA], kr, z, _rot_cols(kr), z], axis=1)
    uq = w_uq.reshape(MLA_Q_LORA, MLA_HEADS, MLA_NOPE + MLA_ROPE)
    rp = uq[:, :, MLA_NOPE:]
    pad = jnp.zeros((MLA_Q_LORA, MLA_HEADS, LANES - MLA_ROPE), F32)
    nh = MLA_HEADS * LANES
    wuq = jnp.concatenate([uq[:, :, :MLA_NOPE].reshape(MLA_Q_LORA, nh),
                           jnp.concatenate([rp, pad], -1).reshape(MLA_Q_LORA, nh),
                           jnp.concatenate([_rot_cols(rp), pad], -1).reshape(MLA_Q_LORA, nh)], axis=1)
    wuq = wuq.astype(BF16)
    return dict(wh=wh.astype(BF16), gq=g_q[None, :], wuq=wuq, wuq_t=wuq.T, gkv=g_kv[None, :],
                wuk=jnp.transpose(w_uk, (1, 2, 0)).astype(BF16),
                wuv=jnp.transpose(w_uv, (1, 0, 2)).astype(BF16),
                wuk_flat=w_uk.reshape(MLA_KV_LORA, MLA_HEADS * MLA_NOPE).astype(BF16),
                wuv_flat_t=w_uv.reshape(MLA_KV_LORA, MLA_HEADS * MLA_V).T.astype(BF16))


def _prep_swa(w_qkv, b_qkv, sinks, w_o):
    nq = SWA_HEADS * SWA_HEAD_DIM
    nk = SWA_KV_HEADS * SWA_HEAD_DIM
    perm = jnp.array(SWA_PERM)

    def split(a):
        lead = a.shape[:-1]
        q = a[..., :nq].reshape(*lead, SWA_HEADS, SWA_HEAD_DIM)[..., perm, :]
        k = a[..., nq:nq + nk].reshape(*lead, SWA_KV_HEADS, SWA_HEAD_DIM)
        return jnp.concatenate([q.reshape(*lead, nq), _rot_cols(q).reshape(*lead, nq),
                                k.reshape(*lead, nk), _rot_cols(k).reshape(*lead, nk), a[..., nq + nk:]], axis=-1)

    wo = w_o.reshape(SWA_HEADS, SWA_HEAD_DIM, D_MODEL)[perm].reshape(nq, D_MODEL)
    return split(w_qkv).astype(BF16), split(b_qkv)[None, :], sinks[perm], wo.astype(BF16)


def _prep_post(w_o, g_ffn, w_in, conv_w, conv_b, w_out, g_ple, w_gate, w_proj):
    def chunks(w):
        return jnp.transpose(w.reshape(D_MODEL, N_FF_CHUNKS, FF_CHUNK), (1, 0, 2)).astype(BF16)
    return dict(wo=w_o.astype(BF16), gf=g_ffn[None, :], wg=chunks(w_in[:, :D_FF]), wu=chunks(w_in[:, D_FF:]),
                cw=conv_w, cb=conv_b[None, :], wout=w_out.reshape(N_FF_CHUNKS, FF_CHUNK, D_MODEL).astype(BF16),
                gp=g_ple[None, :], wpg=w_gate.astype(BF16), wpp=w_proj.astype(BF16))


def kernel(x_prompt, x_sample, p_prompt, p_sample, cache_mla_ckv, cache_mla_krope, cache_sb_k, cache_sb_v, cache_swa_k, cache_swa_v, state_ffn_conv, g_mix, g_ffn, g_ple, g_final, w_mla_dq, g_mla_q, w_mla_uq, w_mla_dkv, g_mla_kv, w_mla_uk, w_mla_uv, w_mla_o, w_sb_qkv, w_sb_o, w_swa_qkv, b_swa_qkv, swa_sinks, w_swa_o, w_ffn_in, ffn_conv_w, ffn_conv_b, w_ffn_out, w_ple_gate, w_ple_proj):
    b, s, _ = x_prompt.shape
    db, t, _ = x_sample.shape
    depth = g_mix.shape[0]
    past = cache_mla_ckv.shape[2]
    n_p = b * s
    n_s = db * t
    assert s % 512 == 0 and t == CHUNK and past % 256 == 0 and n_s % 512 == 0
    assert cache_swa_k.shape[2] == SWA_WINDOW and 512 % t == 0

    tm_mla = 256
    tm = 512
    tt = 512
    ns = 512 // t
    cos_p, sin_p = _rope_tables(jnp.arange(s, dtype=jnp.int32))
    cos_s, sin_s = _rope_tables(past + (jnp.arange(tm, dtype=jnp.int32) % t))
    zero_state = jnp.zeros((b, 2, D_FF), F32)
    g_fin = g_final[None, :]

    xp, xs = x_prompt, x_sample
    outs = {k: [] for k in ("ckv_p", "kr_p", "ckv_s", "kr_s", "sbk_p", "sbv_p", "sbk_s", "sbv_s",
                            "swk_p", "swv_p", "swk_s", "swv_s", "conv_p", "conv_s")}
    for i in range(depth):
        j = i // N_MIXERS
        gm = g_mix[i][None, :]
        xp2 = xp.reshape(n_p, D_MODEL)
        xs2 = xs.reshape(n_s, D_MODEL)
        if i % N_MIXERS == 0:
            w = _prep_mla(w_mla_dq[j], g_mla_q[j], w_mla_uq[j], w_mla_dkv[j], g_mla_kv[j], w_mla_uk[j], w_mla_uv[j])
            q_p, ckv_p, kr_p, k_p, v_p = _mla_project_t(xp2, gm, w, cos_p, sin_p, tm_mla)
            q_s, ckv_s, kr_s, kcat_s = _mla_project(xs2, gm, w, cos_s[:tm_mla], sin_s[:tm_mla], tm_mla)
            o_p = _mla_attention(q_p, k_p, v_p, b, s, 256)
            o_s = _mla_attention_sample(q_s, cache_mla_ckv, cache_mla_krope, kcat_s, w["wuv"], j)
            outs["ckv_p"].append(ckv_p.reshape(b, s, MLA_KV_LORA))
            outs["kr_p"].append(kr_p.reshape(b, s, MLA_ROPE))
            outs["ckv_s"].append(ckv_s.reshape(db, t, MLA_KV_LORA))
            outs["kr_s"].append(kr_s.reshape(db, t, MLA_ROPE))
            w_o = w_mla_o[j]
        elif i % N_MIXERS == 1:
            wq = w_sb_qkv[j].astype(BF16)
            q_t, kb_p, v_t, k_p, v_p = _sb_project_t(xp2, gm, wq, b, tm)
            qkv_s, k_s, v_s = _sb_project(xs2, gm, wq, tm)
            o_p = _sb_attention(q_t, kb_p, v_t, b, s, 256)
            o_s = _sb_attention_sample(qkv_s, cache_sb_k, cache_sb_v, j, db, 256)
            to_cache = lambda a: jnp.transpose(a.reshape(b, SB_HEADS, SB_HEAD_DIM, s), (0, 3, 1, 2))
            outs["sbk_p"].append(to_cache(k_p))
            outs["sbv_p"].append(to_cache(v_p))
            outs["sbk_s"].append(k_s.reshape(db, t, SB_HEADS, SB_HEAD_DIM))
            outs["sbv_s"].append(v_s.reshape(db, t, SB_HEADS, SB_HEAD_DIM))
            w_o = w_sb_o[j]
        else:
            wq, bq, sinks, w_o = _prep_swa(w_swa_qkv[j], b_swa_qkv[j], swa_sinks[j], w_swa_o[j])
            q_p, k_p, v_p, kb_p, vb_p = _swa_project(xp2, gm, wq, bq, cos_p, sin_p, tm)
            q_s, k_s, v_s, kb_s, vb_s = _swa_project(xs2, gm, wq, bq, cos_s, sin_s, tm)
            o_p = _swa_attention(sinks, q_p, kb_p, vb_p, b, s)
            o_s = _swa_attention_sample(sinks, q_s, cache_swa_k, cache_swa_v, kb_s, vb_s, j, db, past)
            kv_shape = (SWA_KV_HEADS, SWA_HEAD_DIM)
            outs["swk_p"].append(k_p.reshape(b, s, *kv_shape)[:, -SWA_WINDOW:])
            outs["swv_p"].append(v_p.reshape(b, s, *kv_shape)[:, -SWA_WINDOW:])
            outs["swk_s"].append(jnp.concatenate([cache_swa_k[j], k_s.reshape(db, t, *kv_shape)], 1)[:, -SWA_WINDOW:])
            outs["swv_s"].append(jnp.concatenate([cache_swa_v[j], v_s.reshape(db, t, *kv_shape)], 1)[:, -SWA_WINDOW:])
        wp = _prep_post(w_o, g_ffn[i], w_ffn_in[i], ffn_conv_w[i], ffn_conv_b[i], w_ffn_out[i],
                        g_ple[i], w_ple_gate[i], w_ple_proj[i])
        final = i == depth - 1
        xp, cp = _post(xp, o_p, wp, zero_state, p_prompt, i, g_fin, 1, tt, final)
        xs, cs = _post(xs, o_s, wp, state_ffn_conv[i], p_sample, i, g_fin, ns, t, final)
        outs["conv_p"].append(cp)
        outs["conv_s"].append(cs)
    order = ("ckv_p", "kr_p", "ckv_s", "kr_s", "sbk_p", "sbv_p", "sbk_s", "sbv_s",
             "swk_p", "swv_p", "swk_s", "swv_s", "conv_p", "conv_s")
    return (xp, xs) + tuple(jnp.stack(outs[k]) for k in order)
```

```python
import functools

import jax
import jax.numpy as jnp
from jax import lax
from jax.experimental import pallas as pl
from jax.experimental.pallas import tpu as pltpu

F32 = jnp.float32
BF16 = jnp.bfloat16

D_MODEL = 1024
CHUNK = 64
N_MIXERS = 3
RMS_EPS = 1e-6
ROPE_THETA = 10000.0
NEG = -1e30
LOG2E = 1.4426950408889634

MLA_HEADS = 16
MLA_NOPE = 128
MLA_ROPE = 64
MLA_V = 128
MLA_Q_LORA = 384
MLA_KV_LORA = 256
MLA_SCALE = (MLA_NOPE + MLA_ROPE) ** -0.5
MLA_QK = MLA_KV_LORA + 128
MLA_KDIM = MLA_NOPE + 128
MLA_HEAD_GROUP = 8

SB_HEADS = 16
SB_HEAD_DIM = 64
SB_SCALE = SB_HEAD_DIM ** -0.5

SWA_HEADS = 16
SWA_KV_HEADS = 4
SWA_GROUP = SWA_HEADS // SWA_KV_HEADS
SWA_HEAD_DIM = 64
SWA_WINDOW = 128
SWA_SCALE = SWA_HEAD_DIM ** -0.5
SWA_PERM = tuple(8 * p + 4 * s + g for p in range(2) for g in range(SWA_GROUP) for s in range(2))

D_FF = 2816
FF_CHUNK = 256
N_FF_CHUNKS = D_FF // FF_CHUNK
PLE_DIM = 256

LANES = 128
VMEM_LIMIT = 56 * 1024 * 1024


def _dot(a, b):
    return jnp.dot(a, b, preferred_element_type=F32)


def _dot_nt(a, b):
    return lax.dot_general(a, b, (((1,), (1,)), ((), ())), preferred_element_type=F32)


def _rms(x, g):
    ms = jnp.mean(x * x, axis=-1, keepdims=True)
    return x * lax.rsqrt(ms + RMS_EPS) * g


def _const_spec(shape):
    nd = len(shape)
    return pl.BlockSpec(shape, lambda *_: (0,) * nd, pipeline_mode=pl.Buffered(1))


def _params(sem):
    return pltpu.CompilerParams(dimension_semantics=sem, vmem_limit_bytes=VMEM_LIMIT)


def _mla_latents(x_ref, g_ref, wh_ref, gq_ref, gkv_ref, cos, sin, ckv_ref, kr_ref, kcat_ref):
    h = _rms(x_ref[...], g_ref[...]).astype(BF16)
    y = _dot(h, wh_ref[...])
    cq = _rms(y[:, :MLA_Q_LORA], gq_ref[...])
    o = MLA_Q_LORA
    ckv = _rms(y[:, o:o + MLA_KV_LORA], gkv_ref[...])
    o += MLA_KV_LORA
    kr = y[:, o:o + LANES] * cos + y[:, o + LANES:o + 2 * LANES] * sin
    ckv_ref[...] = ckv
    kr_ref[...] = kr[:, :MLA_ROPE]
    if kcat_ref is not None:
        kcat_ref[...] = jnp.concatenate([ckv, kr], axis=-1).astype(BF16)
    return cq, ckv, kr


def _mla_proj_t_kernel(x_ref, g_ref, wh_ref, gq_ref, wuq_ref, wuk_ref, wuv_ref, gkv_ref, cos_ref, sin_ref,
                       cos_t_ref, sin_t_ref, q_ref, ckv_ref, kr_ref, k_ref, v_ref):
    cq, ckv, kr = _mla_latents(x_ref, g_ref, wh_ref, gq_ref, gkv_ref, cos_ref[...], sin_ref[...],
                               ckv_ref, kr_ref, None)
    ckv_b = ckv.astype(BF16)
    kn = _dot(ckv_b, wuk_ref[...])
    kr_b = kr.astype(BF16)
    for hd in range(MLA_HEADS):
        k_ref[:, hd * MLA_KDIM:hd * MLA_KDIM + LANES] = kn[:, hd * LANES:(hd + 1) * LANES].astype(BF16)
        k_ref[:, hd * MLA_KDIM + LANES:(hd + 1) * MLA_KDIM] = kr_b
    v_ref[...] = _dot(wuv_ref[...], ckv.T.astype(BF16)).astype(BF16)
    qa = _dot(wuq_ref[...], cq.T.astype(BF16))
    cos = cos_t_ref[...]
    sin = sin_t_ref[...]
    nh = MLA_HEADS * LANES
    for hd in range(MLA_HEADS):
        sl = slice(hd * LANES, (hd + 1) * LANES)
        qr = qa[nh:2 * nh][sl] * cos + qa[2 * nh:][sl] * sin
        q_ref[hd] = (jnp.concatenate([qa[sl], qr], axis=0) * (MLA_SCALE * LOG2E)).astype(BF16)


def _mla_project_t(x2d, g, w, cos, sin, tm):
    n = x2d.shape[0]
    ntab = cos.shape[0] // tm
    row = lambda i: (i, 0)
    col = lambda i: (0, i)
    return pl.pallas_call(
        _mla_proj_t_kernel,
        grid=(n // tm,),
        in_specs=[
            pl.BlockSpec((tm, D_MODEL), row),
            _const_spec((1, D_MODEL)),
            _const_spec(w["wh"].shape),
            _const_spec((1, MLA_Q_LORA)),
            _const_spec(w["wuq_t"].shape),
            _const_spec(w["wuk_flat"].shape),
            _const_spec(w["wuv_flat_t"].shape),
            _const_spec((1, MLA_KV_LORA)),
            pl.BlockSpec((tm, LANES), lambda i: (i % ntab, 0)),
            pl.BlockSpec((tm, LANES), lambda i: (i % ntab, 0)),
            pl.BlockSpec((LANES, tm), lambda i: (0, i % ntab)),
            pl.BlockSpec((LANES, tm), lambda i: (0, i % ntab)),
        ],
        out_specs=[
            pl.BlockSpec((MLA_HEADS, MLA_KDIM, tm), lambda i: (0, 0, i)),
            pl.BlockSpec((tm, MLA_KV_LORA), row),
            pl.BlockSpec((tm, MLA_ROPE), row),
            pl.BlockSpec((tm, MLA_HEADS * MLA_KDIM), row),
            pl.BlockSpec((MLA_HEADS * MLA_V, tm), col),
        ],
        out_shape=[
            jax.ShapeDtypeStruct((MLA_HEADS, MLA_KDIM, n), BF16),
            jax.ShapeDtypeStruct((n, MLA_KV_LORA), F32),
            jax.ShapeDtypeStruct((n, MLA_ROPE), F32),
            jax.ShapeDtypeStruct((n, MLA_HEADS * MLA_KDIM), BF16),
            jax.ShapeDtypeStruct((MLA_HEADS * MLA_V, n), BF16),
        ],
        compiler_params=_params(("parallel",)),
        name="mla_proj_t",
    )(x2d, g, w["wh"], w["gq"], w["wuq_t"], w["wuk_flat"], w["wuv_flat_t"], w["gkv"], cos, sin, cos.T, sin.T)


def _mla_proj_kernel(x_ref, g_ref, wh_ref, gq_ref, wuq_ref, wuk_ref, gkv_ref, cos_ref, sin_ref,
                     q_ref, ckv_ref, kr_ref, kcat_ref):
    cos = cos_ref[...]
    sin = sin_ref[...]
    cq, _, _ = _mla_latents(x_ref, g_ref, wh_ref, gq_ref, gkv_ref, cos, sin, ckv_ref, kr_ref, kcat_ref)
    qa = _dot(cq.astype(BF16), wuq_ref[...])
    nh = MLA_HEADS * LANES
    for hd in range(MLA_HEADS):
        sl = slice(hd * LANES, (hd + 1) * LANES)
        ql = _dot(qa[:, sl].astype(BF16), wuk_ref[hd])
        qr = qa[:, nh:2 * nh][:, sl] * cos + qa[:, 2 * nh:][:, sl] * sin
        q_ref[hd] = (jnp.concatenate([ql, qr], axis=-1) * MLA_SCALE).astype(BF16)


def _mla_project(x2d, g, w, cos, sin, tm):
    n = x2d.shape[0]
    ntab = cos.shape[0] // tm
    row = lambda i: (i, 0)
    tab = lambda i: (i % ntab, 0)
    return pl.pallas_call(
        _mla_proj_kernel,
        grid=(n // tm,),
        in_specs=[
            pl.BlockSpec((tm, D_MODEL), row),
            _const_spec((1, D_MODEL)),
            _const_spec(w["wh"].shape),
            _const_spec((1, MLA_Q_LORA)),
            _const_spec(w["wuq"].shape),
            _const_spec(w["wuk"].shape),
            _const_spec((1, MLA_KV_LORA)),
            pl.BlockSpec((tm, LANES), tab),
            pl.BlockSpec((tm, LANES), tab),
        ],
        out_specs=[
            pl.BlockSpec((MLA_HEADS, tm, MLA_QK), lambda i: (0, i, 0)),
            pl.BlockSpec((tm, MLA_KV_LORA), row),
            pl.BlockSpec((tm, MLA_ROPE), row),
            pl.BlockSpec((tm, MLA_QK), row),
        ],
        out_shape=[
            jax.ShapeDtypeStruct((MLA_HEADS, n, MLA_QK), BF16),
            jax.ShapeDtypeStruct((n, MLA_KV_LORA), F32),
            jax.ShapeDtypeStruct((n, MLA_ROPE), F32),
            jax.ShapeDtypeStruct((n, MLA_QK), BF16),
        ],
        compiler_params=_params(("parallel",)),
        name="mla_proj",
    )(x2d, g, w["wh"], w["gq"], w["wuq"], w["wuk"], w["gkv"], cos, sin)


def _sb_proj_kernel(x_ref, g_ref, w_ref, qkv_ref, k_ref, v_ref):
    h = _rms(x_ref[...], g_ref[...]).astype(BF16)
    y = _dot(h, w_ref[...])
    k_ref[...] = y[:, D_MODEL:2 * D_MODEL]
    v_ref[...] = y[:, 2 * D_MODEL:]
    qkv_ref[:, :D_MODEL] = (y[:, :D_MODEL] * (SB_SCALE * LOG2E)).astype(BF16)
    qkv_ref[:, D_MODEL:] = y[:, D_MODEL:].astype(BF16)


def _sb_proj_t_kernel(x_ref, g_ref, w_ref, q_t_ref, kb_ref, v_t_ref, k_t_ref, vf_t_ref):
    h = _rms(x_ref[...], g_ref[...]).astype(BF16)
    y = _dot(h, w_ref[...])
    k = y[:, D_MODEL:2 * D_MODEL]
    v_t = y[:, 2 * D_MODEL:].T
    k_t_ref[0] = k.T
    vf_t_ref[0] = v_t
    kb_ref[...] = k.astype(BF16)
    q_t_ref[...] = (y[:, :D_MODEL] * (SB_SCALE * LOG2E)).T.astype(BF16)
    v_t_ref[...] = v_t.astype(BF16)


def _sb_project_t(x2d, g, w, b, tm):
    n = x2d.shape[0]
    nt = n // b // tm
    row = lambda i: (i, 0)
    col = lambda i: (0, i)
    per_batch = lambda i: (i // nt, 0, i % nt)
    return pl.pallas_call(
        _sb_proj_t_kernel,
        grid=(n // tm,),
        in_specs=[pl.BlockSpec((tm, D_MODEL), row), _const_spec((1, D_MODEL)), _const_spec(w.shape)],
        out_specs=[pl.BlockSpec((D_MODEL, tm), col), pl.BlockSpec((tm, D_MODEL), row),
                   pl.BlockSpec((D_MODEL, tm), col), pl.BlockSpec((1, D_MODEL, tm), per_batch),
                   pl.BlockSpec((1, D_MODEL, tm), per_batch)],
        out_shape=[jax.ShapeDtypeStruct((D_MODEL, n), BF16), jax.ShapeDtypeStruct((n, D_MODEL), BF16),
                   jax.ShapeDtypeStruct((D_MODEL, n), BF16), jax.ShapeDtypeStruct((b, D_MODEL, n // b), F32),
                   jax.ShapeDtypeStruct((b, D_MODEL, n // b), F32)],
        compiler_params=_params(("parallel",)),
        name="sb_proj_t",
    )(x2d, g, w)


def _sb_project(x2d, g, w, tm):
    n = x2d.shape[0]
    row = lambda i: (i, 0)
    return pl.pallas_call(
        _sb_proj_kernel,
        grid=(n // tm,),
        in_specs=[pl.BlockSpec((tm, D_MODEL), row), _const_spec((1, D_MODEL)), _const_spec(w.shape)],
        out_specs=[pl.BlockSpec((tm, 3 * D_MODEL), row), pl.BlockSpec((tm, D_MODEL), row),
                   pl.BlockSpec((tm, D_MODEL), row)],
        out_shape=[jax.ShapeDtypeStruct((n, 3 * D_MODEL), BF16), jax.ShapeDtypeStruct((n, D_MODEL), F32),
                   jax.ShapeDtypeStruct((n, D_MODEL), F32)],
        compiler_params=_params(("parallel",)),
        name="sb_proj",
    )(x2d, g, w)


def _swa_proj_kernel(x_ref, g_ref, w_ref, b_ref, cos_ref, sin_ref, q_ref, k_ref, v_ref, kb_ref, vb_ref):
    h = _rms(x_ref[...], g_ref[...]).astype(BF16)
    y = _dot(h, w_ref[...]) + b_ref[...]
    cos = cos_ref[...]
    sin = sin_ref[...]
    nq = SWA_HEADS * SWA_HEAD_DIM
    nk = SWA_KV_HEADS * SWA_HEAD_DIM
    for c in range(nq // LANES):
        sl = slice(c * LANES, (c + 1) * LANES)
        q = y[:, :nq][:, sl] * cos + y[:, nq:2 * nq][:, sl] * sin
        q_ref[:, sl] = (q * SWA_SCALE).astype(BF16)
    o = 2 * nq
    for c in range(nk // LANES):
        sl = slice(c * LANES, (c + 1) * LANES)
        k = y[:, o:o + nk][:, sl] * cos + y[:, o + nk:o + 2 * nk][:, sl] * sin
        k_ref[:, sl] = k
        kb_ref[:, sl] = k.astype(BF16)
    v = y[:, o + 2 * nk:]
    v_ref[...] = v
    vb_ref[...] = v.astype(BF16)


def _swa_project(x2d, g, w, b, cos, sin, tm):
    n = x2d.shape[0]
    ntab = cos.shape[0] // tm
    nq = SWA_HEADS * SWA_HEAD_DIM
    nk = SWA_KV_HEADS * SWA_HEAD_DIM
    row = lambda i: (i, 0)
    tab = lambda i: (i % ntab, 0)
    return pl.pallas_call(
        _swa_proj_kernel,
        grid=(n // tm,),
        in_specs=[pl.BlockSpec((tm, D_MODEL), row), _const_spec((1, D_MODEL)), _const_spec(w.shape),
                  _const_spec(b.shape), pl.BlockSpec((tm, LANES), tab), pl.BlockSpec((tm, LANES), tab)],
        out_specs=[pl.BlockSpec((tm, nq), row), pl.BlockSpec((tm, nk), row), pl.BlockSpec((tm, nk), row),
                   pl.BlockSpec((tm, nk), row), pl.BlockSpec((tm, nk), row)],
        out_shape=[jax.ShapeDtypeStruct((n, nq), BF16), jax.ShapeDtypeStruct((n, nk), F32),
                   jax.ShapeDtypeStruct((n, nk), F32), jax.ShapeDtypeStruct((n, nk), BF16),
                   jax.ShapeDtypeStruct((n, nk), BF16)],
        compiler_params=_params(("parallel",)),
        name="swa_proj",
    )(x2d, g, w, b, cos, sin)


def _mla_attn_kernel(q_ref, k_ref, v_ref, o_ref, m_sc, l_sc, acc_sc, *, t):
    qi = pl.program_id(2)
    nh = q_ref.shape[0]
    m_sc[...] = jnp.full(m_sc.shape, NEG, F32)
    l_sc[...] = jnp.zeros(l_sc.shape, F32)
    acc_sc[...] = jnp.zeros(acc_sc.shape, F32)
    k_chunk = lax.broadcasted_iota(jnp.int32, (t, 1), 0) // CHUNK
    q_chunk = lax.broadcasted_iota(jnp.int32, (1, t), 1) // CHUNK
    diag_mask = k_chunk <= q_chunk

    def steps(blocks):
        work = [(j, mask, hd) for j, mask in blocks for hd in range(nh)]

        def scores(i):
            j, _, hd = work[i]
            return _dot(k_ref[0, pl.ds(pl.multiple_of(j * t, t), t), hd * MLA_KDIM:(hd + 1) * MLA_KDIM], q_ref[hd])

        ahead = 5
        pending = [scores(i) for i in range(ahead)]
        for i, (j, mask, hd) in enumerate(work):
            start = pl.multiple_of(j * t, t)
            s = pending.pop(0)
            if i + ahead < len(work):
                pending.append(scores(i + ahead))
            if mask is not None:
                s = jnp.where(mask, s, NEG)
            m_old = m_sc[hd]
            m_new = jnp.maximum(m_old, jnp.max(s, axis=0, keepdims=True))
            alpha = jnp.exp2(m_old - m_new)
            p = jnp.exp2(s - m_new)
            l_sc[hd] = alpha * l_sc[hd] + jnp.sum(p, axis=0, keepdims=True)
            v = v_ref[hd * MLA_V:(hd + 1) * MLA_V, pl.ds(start, t)]
            acc_sc[hd] = alpha * acc_sc[hd] + _dot(v, p.astype(BF16))
            m_sc[hd] = m_new

    def body(jj, _):
        steps([(2 * jj, None), (2 * jj + 1, None)])
        return 0

    lax.fori_loop(0, qi // 2, body, 0)

    @pl.when(qi % 2 == 1)
    def _():
        steps([(qi - 1, None), (qi, diag_mask)])

    @pl.when(qi % 2 == 0)
    def _():
        steps([(qi, diag_mask)])
    for hd in range(nh):
        o_ref[:, hd * MLA_V:(hd + 1) * MLA_V] = (acc_sc[hd] / l_sc[hd]).T.astype(BF16)


def _mla_attention(q_t, k_full, v_t, b, s, t):
    n = b * s
    nq = s // t
    g = MLA_HEAD_GROUP
    return pl.pallas_call(
        functools.partial(_mla_attn_kernel, t=t),
        grid=(b, MLA_HEADS // g, nq),
        in_specs=[
            pl.BlockSpec((g, MLA_KDIM, t), lambda bi, hg, qi: (hg, 0, bi * nq + qi)),
            pl.BlockSpec((1, s, g * MLA_KDIM), lambda bi, hg, qi: (bi, 0, hg)),
            pl.BlockSpec((g * MLA_V, s), lambda bi, hg, qi: (hg, bi)),
        ],
        out_specs=pl.BlockSpec((t, g * MLA_V), lambda bi, hg, qi: (bi * nq + qi, hg)),
        out_shape=jax.ShapeDtypeStruct((n, MLA_HEADS * MLA_V), BF16),
        scratch_shapes=[pltpu.VMEM((g, 1, t), F32), pltpu.VMEM((g, 1, t), F32),
                        pltpu.VMEM((g, MLA_V, t), F32)],
        compiler_params=_params(("parallel", "parallel", "arbitrary")),
        name="mla_attn",
    )(q_t, k_full.reshape(b, s, MLA_HEADS * MLA_KDIM), v_t)


def _mla_attn_sample_kernel(q_ref, cc_ref, cr_ref, kn_ref, wuv_ref, o_ref, *, t, past):
    rows = MLA_HEADS * t
    q = q_ref[...].reshape(rows, MLA_QK)
    ck = cc_ref[0].astype(BF16)
    s_old = (_dot_nt(q[:, :MLA_KV_LORA], ck)
             + _dot(q[:, MLA_KV_LORA:MLA_KV_LORA + MLA_ROPE], cr_ref[0].astype(BF16)))
    kn = kn_ref[...]
    s_new = _dot_nt(q, kn)
    tok = lax.broadcasted_iota(jnp.int32, (rows, 1), 0) & (t - 1)
    k_chunk = (past + lax.broadcasted_iota(jnp.int32, (1, t), 1)) // CHUNK
    s_new = jnp.where(k_chunk <= (past + tok) // CHUNK, s_new, NEG)
    m = jnp.maximum(jnp.max(s_old, axis=-1, keepdims=True), jnp.max(s_new, axis=-1, keepdims=True))
    p_old = jnp.exp(s_old - m)
    p_new = jnp.exp(s_new - m)
    l = jnp.sum(p_old, axis=-1, keepdims=True) + jnp.sum(p_new, axis=-1, keepdims=True)
    acc = _dot(p_old.astype(BF16), ck) + _dot(p_new.astype(BF16), kn[:, :MLA_KV_LORA])
    o_lat = (acc / l).astype(BF16)
    for hd in range(MLA_HEADS):
        o = _dot(o_lat[hd * t:(hd + 1) * t], wuv_ref[hd])
        o_ref[:, hd * MLA_V:(hd + 1) * MLA_V] = o.astype(BF16)


def _mla_attention_sample(q, cache_ckv, cache_kr, kcat, wuv, layer):
    _, db, past, _ = cache_ckv.shape
    t = q.shape[1] // db
    return pl.pallas_call(
        functools.partial(_mla_attn_sample_kernel, t=t, past=past),
        grid=(db,),
        in_specs=[
            pl.BlockSpec((MLA_HEADS, t, MLA_QK), lambda bi: (0, bi, 0)),
            pl.BlockSpec((None, 1, past, MLA_KV_LORA), lambda bi: (layer, bi, 0, 0)),
            pl.BlockSpec((None, 1, MLA_ROPE, past), lambda bi: (layer, bi, 0, 0)),
            pl.BlockSpec((t, MLA_QK), lambda bi: (bi, 0)),
            _const_spec(wuv.shape),
        ],
        out_specs=pl.BlockSpec((t, MLA_HEADS * MLA_V), lambda bi: (bi, 0)),
        out_shape=jax.ShapeDtypeStruct((db * t, MLA_HEADS * MLA_V), BF16),
        compiler_params=_params(("parallel",)),
        name="mla_attn_sample",
    )(q, cache_ckv, jnp.swapaxes(cache_kr, 2, 3), kcat, wuv)


def _later_key_matrix(tk):
    r = lax.broadcasted_iota(jnp.int32, (tk, tk), 0)
    c = lax.broadcasted_iota(jnp.int32, (tk, tk), 1)
    return jnp.where(r > c, 1.0, 0.0).astype(BF16)


def _split_pair(q2):
    lane = lax.broadcasted_iota(jnp.int32, (1, LANES), 1)
    zero = jnp.zeros_like(q2)
    return jnp.where(lane < LANES // 2, q2, zero), jnp.where(lane >= LANES // 2, q2, zero), lane


def _sb_logits_stage(z, mask):
    log_sig = jnp.minimum(z, 0.0) - jnp.log(1.0 + jnp.exp2(jnp.minimum(z, -z))) * LOG2E
    log_not = log_sig - z
    if mask is not None:
        log_not = jnp.where(mask, log_not, 0.0)
    hi = log_not.astype(BF16)
    lo = (log_not - hi.astype(F32)).astype(BF16)
    return log_sig, log_not, hi, lo


def _sb_attn_kernel(q_ref, k_ref, v_ref, o_ref, carry_sc, acc_sc, *, t):
    qi = pl.program_id(1)
    low = lax.broadcasted_iota(jnp.int32, (LANES, 1), 0) < LANES // 2
    r = lax.broadcasted_iota(jnp.int32, (t, t), 0)
    c = lax.broadcasted_iota(jnp.int32, (t, t), 1)
    later = jnp.where(c > r, 1.0, 0.0).astype(BF16)
    later2 = jnp.concatenate([later, later], axis=1)
    diag_mask = r < c
    carry_sc[...] = jnp.zeros(carry_sc.shape, F32)
    acc_sc[...] = jnp.zeros(acc_sc.shape, F32)

    def pick(x, hd):
        zero = jnp.zeros_like(x)
        return jnp.where(low, x, zero) if hd % 2 == 0 else jnp.where(low, zero, x)

    def blocks(items):
        work = [(j, mask, hd) for j, mask in items for hd in range(SB_HEADS)]

        def logits(i):
            j, _, hd = work[i]
            g = hd // 2
            qh = pick(q_ref[g * LANES:(g + 1) * LANES, :], hd)
            start = pl.multiple_of(j * t, t)
            return _dot(k_ref[0, pl.ds(start, t), g * LANES:(g + 1) * LANES], qh)

        def stage1(i, z):
            log_sig, log_not, hi, lo = _sb_logits_stage(z, work[i][1])
            tail = _dot(later2, jnp.concatenate([hi, lo], axis=0))
            return i, log_sig, tail, jnp.sum(log_not, axis=0, keepdims=True)

        def stage2(i, log_sig, tail, colsum):
            j, mask, hd = work[i]
            carry = carry_sc[hd]
            a = jnp.exp2(log_sig + (tail + carry))
            if mask is not None:
                a = jnp.where(mask, a, 0.0)
            carry_sc[hd] = carry + colsum
            g = hd // 2
            vh = pick(v_ref[g * LANES:(g + 1) * LANES, pl.ds(pl.multiple_of(j * t, t), t)], hd)
            acc_sc[g] += _dot(vh, a.astype(BF16))

        ahead = 2
        zs = [logits(i) for i in range(ahead)]
        pending = None
        for i in range(len(work)):
            st = stage1(i, zs.pop(0))
            if i + ahead < len(work):
                zs.append(logits(i + ahead))
            if pending is not None:
                stage2(*pending)
            pending = st
        stage2(*pending)

    @pl.when(qi % 2 == 1)
    def _():
        blocks([(qi, diag_mask), (qi - 1, None)])

    @pl.when(qi % 2 == 0)
    def _():
        blocks([(qi, diag_mask)])

    rest = qi - qi % 2

    def body(jj, _):
        blocks([(rest - 1 - 2 * jj, None), (rest - 2 - 2 * jj, None)])
        return 0

    lax.fori_loop(0, rest // 2, body, 0)
    for g in range(SB_HEADS // 2):
        o_ref[:, g * LANES:(g + 1) * LANES] = acc_sc[g].T.astype(BF16)


def _sb_attention(q_t, kb, v_t, b, s, t):
    n = b * s
    nq = s // t
    return pl.pallas_call(
        functools.partial(_sb_attn_kernel, t=t),
        grid=(b, nq),
        in_specs=[
            pl.BlockSpec((D_MODEL, t), lambda bi, qi: (0, bi * nq + qi)),
            pl.BlockSpec((1, s, D_MODEL), lambda bi, qi: (bi, 0, 0)),
            pl.BlockSpec((D_MODEL, s), lambda bi, qi: (0, bi)),
        ],
        out_specs=pl.BlockSpec((t, D_MODEL), lambda bi, qi: (bi * nq + qi, 0)),
        out_shape=jax.ShapeDtypeStruct((n, D_MODEL), BF16),
        scratch_shapes=[pltpu.VMEM((SB_HEADS, 1, t), F32), pltpu.VMEM((SB_HEADS // 2, LANES, t), F32)],
        compiler_params=_params(("parallel", "arbitrary")),
        name="sb_attn",
    )(q_t, kb.reshape(b, s, D_MODEL), v_t)


def _sb_attn_sample_kernel(q_ref, kn_ref, vn_ref, ck_ref, cv_ref, o_ref, carry_sc, acc_sc, *, t, tk):
    j = pl.program_id(1)
    hdim = SB_HEAD_DIM

    def head(ref, hd):
        return ref[:, hd * hdim:(hd + 1) * hdim]

    def run(logits, weighted, later2, mask, first):
        def stage1(z):
            log_sig, log_not, hi, lo = _sb_logits_stage(z, mask)
            tail = _dot(jnp.concatenate([hi, lo], axis=1), later2)
            return log_sig, tail, jnp.sum(log_not, axis=-1, keepdims=True)

        def stage2(hd, log_sig, tail, rowsum):
            if first:
                a = jnp.exp2(log_sig + tail)
                carry_sc[hd] = rowsum
            else:
                carry = carry_sc[hd]
                a = jnp.exp2(log_sig + (tail + carry))
                carry_sc[hd] = carry + rowsum
            if mask is not None:
                a = jnp.where(mask, a, 0.0)
            o = weighted(hd, a.astype(BF16))
            acc_sc[hd] = o if first else acc_sc[hd] + o

        ahead, lag = 4, 2
        zs = [logits(hd) for hd in range(ahead)]
        pending = []
        for hd in range(SB_HEADS):
            pending.append((hd,) + stage1(zs.pop(0)))
            if hd + ahead < SB_HEADS:
                zs.append(logits(hd + ahead))
            if len(pending) > lag:
                stage2(*pending.pop(0))
        for item in pending:
            stage2(*item)

    def later_pair(n):
        u = _later_key_matrix(n)
        return jnp.concatenate([u, u], axis=0)

    @pl.when(j == 0)
    def _():
        r = lax.broadcasted_iota(jnp.int32, (t, t), 0)
        c = lax.broadcasted_iota(jnp.int32, (t, t), 1)
        run(lambda hd: _dot_nt(head(q_ref, hd), head(kn_ref, hd)),
            lambda hd, a: _dot(a, head(vn_ref, hd)), later_pair(t), c < r, True)

    run(lambda hd: _dot(head(q_ref, hd), ck_ref[0, hd].astype(BF16)),
        lambda hd, a: _dot_nt(a, cv_ref[0, hd].astype(BF16)), later_pair(tk), None, False)

    @pl.when(j == pl.num_programs(1) - 1)
    def _():
        o_ref[...] = jnp.concatenate([acc_sc[hd] for hd in range(SB_HEADS)], axis=-1).astype(BF16)


def _sb_attention_sample(qkv, cache_k, cache_v, layer, db, tk):
    n = qkv.shape[0]
    t = n // db
    past = cache_k.shape[2]
    nblk = past // tk
    cache = lambda bi, j: (layer, bi, 0, 0, nblk - 1 - j)
    cache_k = jnp.transpose(cache_k, (0, 1, 3, 4, 2))
    cache_v = jnp.transpose(cache_v, (0, 1, 3, 4, 2))
    return pl.pallas_call(
        functools.partial(_sb_attn_sample_kernel, t=t, tk=tk),
        grid=(db, nblk),
        in_specs=[
            pl.BlockSpec((t, D_MODEL), lambda bi, j: (bi, 0)),
            pl.BlockSpec((t, D_MODEL), lambda bi, j: (bi, 1)),
            pl.BlockSpec((t, D_MODEL), lambda bi, j: (bi, 2)),
            pl.BlockSpec((None, 1, SB_HEADS, SB_HEAD_DIM, tk), cache),
            pl.BlockSpec((None, 1, SB_HEADS, SB_HEAD_DIM, tk), cache),
        ],
        out_specs=pl.BlockSpec((t, D_MODEL), lambda bi, j: (bi, 0)),
        out_shape=jax.ShapeDtypeStruct((n, D_MODEL), BF16),
        scratch_shapes=[pltpu.VMEM((SB_HEADS, t, 1), F32), pltpu.VMEM((SB_HEADS, t, SB_HEAD_DIM), F32)],
        compiler_params=_params(("parallel", "arbitrary")),
        name="sb_attn_sample",
    )(qkv, qkv, qkv, cache_k, cache_v)


def _swa_heads(sink_ref, q_ref, kband, vband, mask, o_ref):
    lane = lax.broadcasted_iota(jnp.int32, (1, LANES), 1)

    def scores(hd):
        blk, half = divmod(hd, 2)
        pair = blk // SWA_GROUP
        qh = _split_pair(q_ref[:, blk * LANES:(blk + 1) * LANES])[half]
        return _dot_nt(qh, kband[:, pair * LANES:(pair + 1) * LANES])

    ahead = 3
    pending = [scores(hd) for hd in range(ahead)]
    outs = []
    for hd in range(SWA_HEADS):
        blk, half = divmod(hd, 2)
        pair = blk // SWA_GROUP
        s = jnp.where(mask, pending.pop(0), NEG)
        if hd + ahead < SWA_HEADS:
            pending.append(scores(hd + ahead))
        sink = sink_ref[hd]
        m = jnp.maximum(jnp.max(s, axis=-1, keepdims=True), sink)
        e = jnp.exp(s - m)
        den = jnp.sum(e, axis=-1, keepdims=True) + jnp.exp(sink - m)
        outs.append(_dot((e / den).astype(BF16), vband[:, pair * LANES:(pair + 1) * LANES]))
        if half:
            o_ref[:, blk * LANES:(blk + 1) * LANES] = jnp.where(lane < LANES // 2, outs[-2], outs[-1]).astype(BF16)


def _swa_attn_kernel(sink_ref, q_ref, kp_ref, kc_ref, vp_ref, vc_ref, o_ref, *, tq):
    i = pl.program_id(1)
    kband = jnp.concatenate([kp_ref[0], kc_ref[0]], axis=0)
    vband = jnp.concatenate([vp_ref[0], vc_ref[0]], axis=0)
    q_pos = i * tq + lax.broadcasted_iota(jnp.int32, (tq, 1), 0)
    k_pos = i * tq - SWA_WINDOW + lax.broadcasted_iota(jnp.int32, (1, SWA_WINDOW + tq), 1)
    dc = (q_pos + SWA_WINDOW) // CHUNK - (k_pos + SWA_WINDOW) // CHUNK
    mask = (dc >= 0) & (dc <= SWA_WINDOW // CHUNK) & (k_pos >= 0)
    _swa_heads(sink_ref, q_ref, kband, vband, mask, o_ref)


def _swa_attention(sinks, q, kb, vb, b, s):
    tq = SWA_WINDOW
    nq = s // tq
    nk = SWA_KV_HEADS * SWA_HEAD_DIM
    kb3 = kb.reshape(b, s, nk)
    vb3 = vb.reshape(b, s, nk)
    prev = lambda bi, qi: (bi, jnp.maximum(qi - 1, 0), 0)
    cur = lambda bi, qi: (bi, qi, 0)
    return pl.pallas_call(
        functools.partial(_swa_attn_kernel, tq=tq),
        grid=(b, nq),
        in_specs=[
            pl.BlockSpec(memory_space=pltpu.SMEM),
            pl.BlockSpec((tq, D_MODEL), lambda bi, qi: (bi * nq + qi, 0)),
            pl.BlockSpec((1, tq, nk), prev), pl.BlockSpec((1, tq, nk), cur),
            pl.BlockSpec((1, tq, nk), prev), pl.BlockSpec((1, tq, nk), cur),
        ],
        out_specs=pl.BlockSpec((tq, D_MODEL), lambda bi, qi: (bi * nq + qi, 0)),
        out_shape=jax.ShapeDtypeStruct((b * s, D_MODEL), BF16),
        compiler_params=_params(("parallel", "arbitrary")),
        name="swa_attn",
    )(sinks, q, kb3, kb3, vb3, vb3)


def _swa_attn_sample_kernel(sink_ref, q_ref, ck_ref, cv_ref, kn_ref, vn_ref, o_ref, *, t, past):
    clen = ck_ref.shape[1]
    kband = jnp.concatenate([ck_ref[0].astype(BF16), kn_ref[...]], axis=0)
    vband = jnp.concatenate([cv_ref[0].astype(BF16), vn_ref[...]], axis=0)
    q_pos = past + lax.broadcasted_iota(jnp.int32, (t, 1), 0)
    k_pos = past - clen + lax.broadcasted_iota(jnp.int32, (1, clen + t), 1)
    dc = q_pos // CHUNK - (k_pos + SWA_WINDOW) // CHUNK + SWA_WINDOW // CHUNK
    mask = (dc >= 0) & (dc <= SWA_WINDOW // CHUNK) & (k_pos >= 0)
    _swa_heads(sink_ref, q_ref, kband, vband, mask, o_ref)


def _swa_attention_sample(sinks, q, cache_k, cache_v, kb, vb, layer, db, past):
    n = q.shape[0]
    t = n // db
    clen = cache_k.shape[2]
    nk = SWA_KV_HEADS * SWA_HEAD_DIM
    ck = cache_k.reshape(cache_k.shape[0], db, clen, nk)
    cv = cache_v.reshape(cache_v.shape[0], db, clen, nk)
    row = lambda bi: (bi, 0)
    cache = lambda bi: (layer, bi, 0, 0)
    return pl.pallas_call(
        functools.partial(_swa_attn_sample_kernel, t=t, past=past),
        grid=(db,),
        in_specs=[
            pl.BlockSpec(memory_space=pltpu.SMEM),
            pl.BlockSpec((t, D_MODEL), row),
            pl.BlockSpec((None, 1, clen, nk), cache), pl.BlockSpec((None, 1, clen, nk), cache),
            pl.BlockSpec((t, nk), row), pl.BlockSpec((t, nk), row),
        ],
        out_specs=pl.BlockSpec((t, D_MODEL), row),
        out_shape=jax.ShapeDtypeStruct((n, D_MODEL), BF16),
        compiler_params=_params(("parallel",)),
        name="swa_attn_sample",
    )(sinks, q, ck, cv, kb, vb)


def _post_kernel(x_ref, o_ref, wo_ref, gf_ref, wg_ref, wu_ref, cw_ref, cb_ref, wout_ref, st_ref,
                 p_ref, gp_ref, wpg_ref, wpp_ref, gfin_ref, y_ref, cs_ref, carry_sc, *, final):
    ti = pl.program_id(1)
    ns, tt, _ = x_ref.shape
    rows = ns * tt
    x = x_ref[...].reshape(rows, D_MODEL)
    x = x + _dot(o_ref[...].reshape(rows, o_ref.shape[-1]), wo_ref[...])
    h = _rms(x, gf_ref[...]).astype(BF16)
    t_idx = lax.broadcasted_iota(jnp.int32, (1, tt, 1), 1)

    @pl.when(ti == 0)
    def _():
        carry_sc[...] = st_ref[...]

    proj = _dot(p_ref[...].reshape(rows, PLE_DIM).astype(BF16), wpp_ref[...])
    acc = jnp.zeros((rows, D_MODEL), F32)
    ahead = 2
    nxt = [(_dot(h, wg_ref[c]), _dot(h, wu_ref[c])) for c in range(ahead)]
    for c in range(N_FF_CHUNKS):
        sl = slice(c * FF_CHUNK, (c + 1) * FF_CHUNK)
        gate, up = nxt.pop(0)
        if c + ahead < N_FF_CHUNKS:
            nxt.append((_dot(h, wg_ref[c + ahead]), _dot(h, wu_ref[c + ahead])))
        st = carry_sc[:, :, sl]
        g1 = pltpu.roll(gate, 1, 0).reshape(ns, tt, FF_CHUNK)
        g2 = pltpu.roll(gate, 2, 0).reshape(ns, tt, FF_CHUNK)
        gate = gate.reshape(ns, tt, FF_CHUNK)
        g1 = jnp.where(t_idx == 0, st[:, 1:2, :], g1)
        g2 = jnp.where(t_idx == 0, st[:, 0:1, :], jnp.where(t_idx == 1, st[:, 1:2, :], g2))
        cw = cw_ref[:, sl]
        conv = g2 * cw[0:1, :] + g1 * cw[1:2, :] + gate * cw[2:3, :] + cb_ref[:, sl]
        new_st = gate[:, tt - 2:, :]
        carry_sc[:, :, sl] = new_st
        cs_ref[:, :, sl] = new_st
        act = 0.5 * conv * (1.0 + lax.erf(conv * (0.5 ** 0.5)))
        y = (act * up.reshape(ns, tt, FF_CHUNK)).reshape(rows, FF_CHUNK)
        acc = acc + _dot(y.astype(BF16), wout_ref[c])
    x = x + acc
    hp = _rms(x, gp_ref[...]).astype(BF16)
    gate_p = jax.nn.sigmoid(_dot(hp, wpg_ref[...]))
    x = x + gate_p * proj
    if final:
        x = _rms(x, gfin_ref[...])
    y_ref[...] = x.reshape(ns, tt, D_MODEL)


def _post(x3, o2d, w, state, p4, layer, g_final, ns, tt, final):
    bx, tx, _ = x3.shape
    do = o2d.shape[-1]
    o3 = o2d.reshape(bx, tx, do)
    blk = lambda bi, ti: (bi, ti, 0)
    return pl.pallas_call(
        functools.partial(_post_kernel, final=final),
        grid=(bx // ns, tx // tt),
        in_specs=[
            pl.BlockSpec((ns, tt, D_MODEL), blk),
            pl.BlockSpec((ns, tt, do), blk),
            _const_spec(w["wo"].shape),
            _const_spec((1, D_MODEL)),
            _const_spec(w["wg"].shape),
            _const_spec(w["wu"].shape),
            _const_spec(w["cw"].shape),
            _const_spec(w["cb"].shape),
            _const_spec(w["wout"].shape),
            pl.BlockSpec((ns, 2, D_FF), lambda bi, ti: (bi, 0, 0)),
            pl.BlockSpec((None, ns, tt, PLE_DIM), lambda bi, ti: (layer, bi, ti, 0)),
            _const_spec((1, D_MODEL)),
            _const_spec(w["wpg"].shape),
            _const_spec(w["wpp"].shape),
            _const_spec((1, D_MODEL)),
        ],
        out_specs=[pl.BlockSpec((ns, tt, D_MODEL), blk),
                   pl.BlockSpec((ns, 2, D_FF), lambda bi, ti: (bi, 0, 0))],
        out_shape=[jax.ShapeDtypeStruct(x3.shape, F32), jax.ShapeDtypeStruct((bx, 2, D_FF), F32)],
        scratch_shapes=[pltpu.VMEM((ns, 2, D_FF), F32)],
        compiler_params=_params(("parallel", "arbitrary")),
        name="post_ffn",
    )(x3, o3, w["wo"], w["gf"], w["wg"], w["wu"], w["cw"], w["cb"], w["wout"], state, p4,
      w["gp"], w["wpg"], w["wpp"], g_final)


def _rot_cols(w):
    half = w.shape[-1] // 2
    return jnp.concatenate([-w[..., half:], w[..., :half]], axis=-1)


def _rope_tables(pos):
    half = MLA_ROPE // 2
    inv = ROPE_THETA ** (-jnp.arange(half, dtype=F32) / half)
    ang = pos.astype(F32)[:, None] * inv[None, :]
    reps = LANES // half
    return jnp.tile(jnp.cos(ang), (1, reps)), jnp.tile(jnp.sin(ang), (1, reps))


def _prep_mla(w_dq, g_q, w_uq, w_dkv, g_kv, w_uk, w_uv):
    z = jnp.zeros((D_MODEL, LANES - MLA_ROPE), F32)
    kr = w_dkv[:, MLA_KV_LORA:]
    wh = jnp.concatenate([w_dq, w_dkv[:, :MLA_KV_LORA], kr, z, _rot_cols(kr), z], axis=1)
    uq = w_uq.reshape(MLA_Q_LORA, MLA_HEADS, MLA_NOPE + MLA_ROPE)
    rp = uq[:, :, MLA_NOPE:]
    pad = jnp.zeros((MLA_Q_LORA, MLA_HEADS, LANES - MLA_ROPE), F32)
    nh = MLA_HEADS * LANES
    wuq = jnp.concatenate([uq[:, :, :MLA_NOPE].reshape(MLA_Q_LORA, nh),
                           jnp.concatenate([rp, pad], -1).reshape(MLA_Q_LORA, nh),
                           jnp.concatenate([_rot_cols(rp), pad], -1).reshape(MLA_Q_LORA, nh)], axis=1)
    wuq = wuq.astype(BF16)
    return dict(wh=wh.astype(BF16), gq=g_q[None, :], wuq=wuq, wuq_t=wuq.T, gkv=g_kv[None, :],
                wuk=jnp.transpose(w_uk, (1, 2, 0)).astype(BF16),
                wuv=jnp.transpose(w_uv, (1, 0, 2)).astype(BF16),
                wuk_flat=w_uk.reshape(MLA_KV_LORA, MLA_HEADS * MLA_NOPE).astype(BF16),
                wuv_flat_t=w_uv.reshape(MLA_KV_LORA, MLA_HEADS * MLA_V).T.astype(BF16))


def _prep_swa(w_qkv, b_qkv, sinks, w_o):
    nq = SWA_HEADS * SWA_HEAD_DIM
    nk = SWA_KV_HEADS * SWA_HEAD_DIM
    perm = jnp.array(SWA_PERM)

    def split(a):
        lead = a.shape[:-1]
        q = a[..., :nq].reshape(*lead, SWA_HEADS, SWA_HEAD_DIM)[..., perm, :]
        k = a[..., nq:nq + nk].reshape(*lead, SWA_KV_HEADS, SWA_HEAD_DIM)
        return jnp.concatenate([q.reshape(*lead, nq), _rot_cols(q).reshape(*lead, nq),
                                k.reshape(*lead, nk), _rot_cols(k).reshape(*lead, nk), a[..., nq + nk:]], axis=-1)

    wo = w_o.reshape(SWA_HEADS, SWA_HEAD_DIM, D_MODEL)[perm].reshape(nq, D_MODEL)
    return split(w_qkv).astype(BF16), split(b_qkv)[None, :], sinks[perm], wo.astype(BF16)


def _prep_post(w_o, g_ffn, w_in, conv_w, conv_b, w_out, g_ple, w_gate, w_proj):
    def chunks(w):
        return jnp.transpose(w.reshape(D_MODEL, N_FF_CHUNKS, FF_CHUNK), (1, 0, 2)).astype(BF16)
    return dict(wo=w_o.astype(BF16), gf=g_ffn[None, :], wg=chunks(w_in[:, :D_FF]), wu=chunks(w_in[:, D_FF:]),
                cw=conv_w, cb=conv_b[None, :], wout=w_out.reshape(N_FF_CHUNKS, FF_CHUNK, D_MODEL).astype(BF16),
                gp=g_ple[None, :], wpg=w_gate.astype(BF16), wpp=w_proj.astype(BF16))


def kernel(x_prompt, x_sample, p_prompt, p_sample, cache_mla_ckv, cache_mla_krope, cache_sb_k, cache_sb_v, cache_swa_k, cache_swa_v, state_ffn_conv, g_mix, g_ffn, g_ple, g_final, w_mla_dq, g_mla_q, w_mla_uq, w_mla_dkv, g_mla_kv, w_mla_uk, w_mla_uv, w_mla_o, w_sb_qkv, w_sb_o, w_swa_qkv, b_swa_qkv, swa_sinks, w_swa_o, w_ffn_in, ffn_conv_w, ffn_conv_b, w_ffn_out, w_ple_gate, w_ple_proj):
    b, s, _ = x_prompt.shape
    db, t, _ = x_sample.shape
    depth = g_mix.shape[0]
    past = cache_mla_ckv.shape[2]
    n_p = b * s
    n_s = db * t
    assert s % 512 == 0 and t == CHUNK and past % 256 == 0 and n_s % 512 == 0
    assert cache_swa_k.shape[2] == SWA_WINDOW and 512 % t == 0

    tm_mla = 256
    tm = 512
    tt = 512
    ns = 512 // t
    cos_p, sin_p = _rope_tables(jnp.arange(s, dtype=jnp.int32))
    cos_s, sin_s = _rope_tables(past + (jnp.arange(tm, dtype=jnp.int32) % t))
    zero_state = jnp.zeros((b, 2, D_FF), F32)
    g_fin = g_final[None, :]

    xp, xs = x_prompt, x_sample
    outs = {k: [] for k in ("ckv_p", "kr_p", "ckv_s", "kr_s", "sbk_p", "sbv_p", "sbk_s", "sbv_s",
                            "swk_p", "swv_p", "swk_s", "swv_s", "conv_p", "conv_s")}
    for i in range(depth):
        j = i // N_MIXERS
        gm = g_mix[i][None, :]
        xp2 = xp.reshape(n_p, D_MODEL)
        xs2 = xs.reshape(n_s, D_MODEL)
        if i % N_MIXERS == 0:
            w = _prep_mla(w_mla_dq[j], g_mla_q[j], w_mla_uq[j], w_mla_dkv[j], g_mla_kv[j], w_mla_uk[j], w_mla_uv[j])
            q_p, ckv_p, kr_p, k_p, v_p = _mla_project_t(xp2, gm, w, cos_p, sin_p, tm_mla)
            q_s, ckv_s, kr_s, kcat_s = _mla_project(xs2, gm, w, cos_s[:tm_mla], sin_s[:tm_mla], tm_mla)
            o_p = _mla_attention(q_p, k_p, v_p, b, s, 256)
            o_s = _mla_attention_sample(q_s, cache_mla_ckv, cache_mla_krope, kcat_s, w["wuv"], j)
            outs["ckv_p"].append(ckv_p.reshape(b, s, MLA_KV_LORA))
            outs["kr_p"].append(kr_p.reshape(b, s, MLA_ROPE))
            outs["ckv_s"].append(ckv_s.reshape(db, t, MLA_KV_LORA))
            outs["kr_s"].append(kr_s.reshape(db, t, MLA_ROPE))
            w_o = w_mla_o[j]
        elif i % N_MIXERS == 1:
            wq = w_sb_qkv[j].astype(BF16)
            q_t, kb_p, v_t, k_p, v_p = _sb_project_t(xp2, gm, wq, b, tm)
            qkv_s, k_s, v_s = _sb_project(xs2, gm, wq, tm)
            o_p = _sb_attention(q_t, kb_p, v_t, b, s, 256)
            o_s = _sb_attention_sample(qkv_s, cache_sb_k, cache_sb_v, j, db, 256)
            to_cache = lambda a: jnp.transpose(a.reshape(b, SB_HEADS, SB_HEAD_DIM, s), (0, 3, 1, 2))
            outs["sbk_p"].append(to_cache(k_p))
            outs["sbv_p"].append(to_cache(v_p))
            outs["sbk_s"].append(k_s.reshape(db, t, SB_HEADS, SB_HEAD_DIM))
            outs["sbv_s"].append(v_s.reshape(db, t, SB_HEADS, SB_HEAD_DIM))
            w_o = w_sb_o[j]
        else:
            wq, bq, sinks, w_o = _prep_swa(w_swa_qkv[j], b_swa_qkv[j], swa_sinks[j], w_swa_o[j])
            q_p, k_p, v_p, kb_p, vb_p = _swa_project(xp2, gm, wq, bq, cos_p, sin_p, tm)
            q_s, k_s, v_s, kb_s, vb_s = _swa_project(xs2, gm, wq, bq, cos_s, sin_s, tm)
            o_p = _swa_attention(sinks, q_p, kb_p, vb_p, b, s)
            o_s = _swa_attention_sample(sinks, q_s, cache_swa_k, cache_swa_v, kb_s, vb_s, j, db, past)
            kv_shape = (SWA_KV_HEADS, SWA_HEAD_DIM)
            outs["swk_p"].append(k_p.reshape(b, s, *kv_shape)[:, -SWA_WINDOW:])
            outs["swv_p"].append(v_p.reshape(b, s, *kv_shape)[:, -SWA_WINDOW:])
            outs["swk_s"].append(jnp.concatenate([cache_swa_k[j], k_s.reshape(db, t, *kv_shape)], 1)[:, -SWA_WINDOW:])
            outs["swv_s"].append(jnp.concatenate([cache_swa_v[j], v_s.reshape(db, t, *kv_shape)], 1)[:, -SWA_WINDOW:])
        wp = _prep_post(w_o, g_ffn[i], w_ffn_in[i], ffn_conv_w[i], ffn_conv_b[i], w_ffn_out[i],
                        g_ple[i], w_ple_gate[i], w_ple_proj[i])
        final = i == depth - 1
        xp, cp = _post(xp, o_p, wp, zero_state, p_prompt, i, g_fin, 1, tt, final)
        xs, cs = _post(xs, o_s, wp, state_ffn_conv[i], p_sample, i, g_fin, ns, t, final)
        outs["conv_p"].append(cp)
        outs["conv_s"].append(cs)
    order = ("ckv_p", "kr_p", "ckv_s", "kr_s", "sbk_p", "sbv_p", "sbk_s", "sbv_s",
             "swk_p", "swv_p", "swk_s", "swv_s", "conv_p", "conv_s")
    return (xp, xs) + tuple(jnp.stack(outs[k]) for k in order)
```

```python
import functools

import jax
import jax.numpy as jnp
from jax import lax
from jax.experimental import pallas as pl
from jax.experimental.pallas import tpu as pltpu

F32 = jnp.float32
BF16 = jnp.bfloat16

D_MODEL = 1024
CHUNK = 64
N_MIXERS = 3
RMS_EPS = 1e-6
ROPE_THETA = 10000.0
NEG = -1e30
LOG2E = 1.4426950408889634

MLA_HEADS = 16
MLA_NOPE = 128
MLA_ROPE = 64
MLA_V = 128
MLA_Q_LORA = 384
MLA_KV_LORA = 256
MLA_SCALE = (MLA_NOPE + MLA_ROPE) ** -0.5
MLA_QK = MLA_KV_LORA + 128
MLA_KDIM = MLA_NOPE + 128
MLA_HEAD_GROUP = 8

SB_HEADS = 16
SB_HEAD_DIM = 64
SB_SCALE = SB_HEAD_DIM ** -0.5

SWA_HEADS = 16
SWA_KV_HEADS = 4
SWA_GROUP = SWA_HEADS // SWA_KV_HEADS
SWA_HEAD_DIM = 64
SWA_WINDOW = 128
SWA_SCALE = SWA_HEAD_DIM ** -0.5
SWA_PERM = tuple(8 * p + 4 * s + g for p in range(2) for g in range(SWA_GROUP) for s in range(2))

D_FF = 2816
FF_CHUNK = 256
N_FF_CHUNKS = D_FF // FF_CHUNK
PLE_DIM = 256

LANES = 128
VMEM_LIMIT = 56 * 1024 * 1024


def _dot(a, b):
    return jnp.dot(a, b, preferred_element_type=F32)


def _dot_nt(a, b):
    return lax.dot_general(a, b, (((1,), (1,)), ((), ())), preferred_element_type=F32)


def _rms(x, g):
    ms = jnp.mean(x * x, axis=-1, keepdims=True)
    return x * lax.rsqrt(ms + RMS_EPS) * g


def _const_spec(shape):
    nd = len(shape)
    return pl.BlockSpec(shape, lambda *_: (0,) * nd, pipeline_mode=pl.Buffered(1))


def _params(sem):
    return pltpu.CompilerParams(dimension_semantics=sem, vmem_limit_bytes=VMEM_LIMIT)


def _mla_latents(x_ref, g_ref, wh_ref, gq_ref, gkv_ref, cos, sin, ckv_ref, kr_ref, kcat_ref):
    h = _rms(x_ref[...], g_ref[...]).astype(BF16)
    y = _dot(h, wh_ref[...])
    cq = _rms(y[:, :MLA_Q_LORA], gq_ref[...])
    o = MLA_Q_LORA
    ckv = _rms(y[:, o:o + MLA_KV_LORA], gkv_ref[...])
    o += MLA_KV_LORA
    kr = y[:, o:o + LANES] * cos + y[:, o + LANES:o + 2 * LANES] * sin
    ckv_ref[...] = ckv
    kr_ref[...] = kr[:, :MLA_ROPE]
    if kcat_ref is not None:
        kcat_ref[...] = jnp.concatenate([ckv, kr], axis=-1).astype(BF16)
    return cq, ckv, kr


def _mla_proj_t_kernel(x_ref, g_ref, wh_ref, gq_ref, wuq_ref, wuk_ref, wuv_ref, gkv_ref, cos_ref, sin_ref,
                       cos_t_ref, sin_t_ref, q_ref, ckv_ref, kr_ref, k_ref, v_ref):
    cq, ckv, kr = _mla_latents(x_ref, g_ref, wh_ref, gq_ref, gkv_ref, cos_ref[...], sin_ref[...],
                               ckv_ref, kr_ref, None)
    ckv_b = ckv.astype(BF16)
    kn = _dot(ckv_b, wuk_ref[...])
    kr_b = kr.astype(BF16)
    for hd in range(MLA_HEADS):
        k_ref[:, hd * MLA_KDIM:hd * MLA_KDIM + LANES] = kn[:, hd * LANES:(hd + 1) * LANES].astype(BF16)
        k_ref[:, hd * MLA_KDIM + LANES:(hd + 1) * MLA_KDIM] = kr_b
    v_ref[...] = _dot(wuv_ref[...], ckv.T.astype(BF16)).astype(BF16)
    qa = _dot(wuq_ref[...], cq.T.astype(BF16))
    cos = cos_t_ref[...]
    sin = sin_t_ref[...]
    nh = MLA_HEADS * LANES
    for hd in range(MLA_HEADS):
        sl = slice(hd * LANES, (hd + 1) * LANES)
        qr = qa[nh:2 * nh][sl] * cos + qa[2 * nh:][sl] * sin
        q_ref[hd] = (jnp.concatenate([qa[sl], qr], axis=0) * (MLA_SCALE * LOG2E)).astype(BF16)


def _mla_project_t(x2d, g, w, cos, sin, tm):
    n = x2d.shape[0]
    ntab = cos.shape[0] // tm
    row = lambda i: (i, 0)
    col = lambda i: (0, i)
    return pl.pallas_call(
        _mla_proj_t_kernel,
        grid=(n // tm,),
        in_specs=[
            pl.BlockSpec((tm, D_MODEL), row),
            _const_spec((1, D_MODEL)),
            _const_spec(w["wh"].shape),
            _const_spec((1, MLA_Q_LORA)),
            _const_spec(w["wuq_t"].shape),
            _const_spec(w["wuk_flat"].shape),
            _const_spec(w["wuv_flat_t"].shape),
            _const_spec((1, MLA_KV_LORA)),
            pl.BlockSpec((tm, LANES), lambda i: (i % ntab, 0)),
            pl.BlockSpec((tm, LANES), lambda i: (i % ntab, 0)),
            pl.BlockSpec((LANES, tm), lambda i: (0, i % ntab)),
            pl.BlockSpec((LANES, tm), lambda i: (0, i % ntab)),
        ],
        out_specs=[
            pl.BlockSpec((MLA_HEADS, MLA_KDIM, tm), lambda i: (0, 0, i)),
            pl.BlockSpec((tm, MLA_KV_LORA), row),
            pl.BlockSpec((tm, MLA_ROPE), row),
            pl.BlockSpec((tm, MLA_HEADS * MLA_KDIM), row),
            pl.BlockSpec((MLA_HEADS * MLA_V, tm), col),
        ],
        out_shape=[
            jax.ShapeDtypeStruct((MLA_HEADS, MLA_KDIM, n), BF16),
            jax.ShapeDtypeStruct((n, MLA_KV_LORA), F32),
            jax.ShapeDtypeStruct((n, MLA_ROPE), F32),
            jax.ShapeDtypeStruct((n, MLA_HEADS * MLA_KDIM), BF16),
            jax.ShapeDtypeStruct((MLA_HEADS * MLA_V, n), BF16),
        ],
        compiler_params=_params(("parallel",)),
        name="mla_proj_t",
    )(x2d, g, w["wh"], w["gq"], w["wuq_t"], w["wuk_flat"], w["wuv_flat_t"], w["gkv"], cos, sin, cos.T, sin.T)


def _mla_proj_kernel(x_ref, g_ref, wh_ref, gq_ref, wuq_ref, wuk_ref, gkv_ref, cos_ref, sin_ref,
                     q_ref, ckv_ref, kr_ref, kcat_ref):
    cos = cos_ref[...]
    sin = sin_ref[...]
    cq, _, _ = _mla_latents(x_ref, g_ref, wh_ref, gq_ref, gkv_ref, cos, sin, ckv_ref, kr_ref, kcat_ref)
    qa = _dot(cq.astype(BF16), wuq_ref[...])
    nh = MLA_HEADS * LANES
    for hd in range(MLA_HEADS):
        sl = slice(hd * LANES, (hd + 1) * LANES)
        ql = _dot(qa[:, sl].astype(BF16), wuk_ref[hd])
        qr = qa[:, nh:2 * nh][:, sl] * cos + qa[:, 2 * nh:][:, sl] * sin
        q_ref[hd] = (jnp.concatenate([ql, qr], axis=-1) * MLA_SCALE).astype(BF16)


def _mla_project(x2d, g, w, cos, sin, tm):
    n = x2d.shape[0]
    ntab = cos.shape[0] // tm
    row = lambda i: (i, 0)
    tab = lambda i: (i % ntab, 0)
    return pl.pallas_call(
        _mla_proj_kernel,
        grid=(n // tm,),
        in_specs=[
            pl.BlockSpec((tm, D_MODEL), row),
            _const_spec((1, D_MODEL)),
            _const_spec(w["wh"].shape),
            _const_spec((1, MLA_Q_LORA)),
            _const_spec(w["wuq"].shape),
            _const_spec(w["wuk"].shape),
            _const_spec((1, MLA_KV_LORA)),
            pl.BlockSpec((tm, LANES), tab),
            pl.BlockSpec((tm, LANES), tab),
        ],
        out_specs=[
            pl.BlockSpec((MLA_HEADS, tm, MLA_QK), lambda i: (0, i, 0)),
            pl.BlockSpec((tm, MLA_KV_LORA), row),
            pl.BlockSpec((tm, MLA_ROPE), row),
            pl.BlockSpec((tm, MLA_QK), row),
        ],
        out_shape=[
            jax.ShapeDtypeStruct((MLA_HEADS, n, MLA_QK), BF16),
            jax.ShapeDtypeStruct((n, MLA_KV_LORA), F32),
            jax.ShapeDtypeStruct((n, MLA_ROPE), F32),
            jax.ShapeDtypeStruct((n, MLA_QK), BF16),
        ],
        compiler_params=_params(("parallel",)),
        name="mla_proj",
    )(x2d, g, w["wh"], w["gq"], w["wuq"], w["wuk"], w["gkv"], cos, sin)


def _sb_proj_kernel(x_ref, g_ref, w_ref, qkv_ref, k_ref, v_ref):
    h = _rms(x_ref[...], g_ref[...]).astype(BF16)
    y = _dot(h, w_ref[...])
    k_ref[...] = y[:, D_MODEL:2 * D_MODEL]
    v_ref[...] = y[:, 2 * D_MODEL:]
    qkv_ref[:, :D_MODEL] = (y[:, :D_MODEL] * (SB_SCALE * LOG2E)).astype(BF16)
    qkv_ref[:, D_MODEL:] = y[:, D_MODEL:].astype(BF16)


def _sb_proj_t_kernel(x_ref, g_ref, w_ref, q_t_ref, kb_ref, v_t_ref, k_t_ref, vf_t_ref):
    h = _rms(x_ref[...], g_ref[...]).astype(BF16)
    y = _dot(h, w_ref[...])
    k = y[:, D_MODEL:2 * D_MODEL]
    v_t = y[:, 2 * D_MODEL:].T
    k_t_ref[0] = k.T
    vf_t_ref[0] = v_t
    kb_ref[...] = k.astype(BF16)
    q_t_ref[...] = (y[:, :D_MODEL] * (SB_SCALE * LOG2E)).T.astype(BF16)
    v_t_ref[...] = v_t.astype(BF16)


def _sb_project_t(x2d, g, w, b, tm):
    n = x2d.shape[0]
    nt = n // b // tm
    row = lambda i: (i, 0)
    col = lambda i: (0, i)
    per_batch = lambda i: (i // nt, 0, i % nt)
    return pl.pallas_call(
        _sb_proj_t_kernel,
        grid=(n // tm,),
        in_specs=[pl.BlockSpec((tm, D_MODEL), row), _const_spec((1, D_MODEL)), _const_spec(w.shape)],
        out_specs=[pl.BlockSpec((D_MODEL, tm), col), pl.BlockSpec((tm, D_MODEL), row),
                   pl.BlockSpec((D_MODEL, tm), col), pl.BlockSpec((1, D_MODEL, tm), per_batch),
                   pl.BlockSpec((1, D_MODEL, tm), per_batch)],
        out_shape=[jax.ShapeDtypeStruct((D_MODEL, n), BF16), jax.ShapeDtypeStruct((n, D_MODEL), BF16),
                   jax.ShapeDtypeStruct((D_MODEL, n), BF16), jax.ShapeDtypeStruct((b, D_MODEL, n // b), F32),
                   jax.ShapeDtypeStruct((b, D_MODEL, n // b), F32)],
        compiler_params=_params(("parallel",)),
        name="sb_proj_t",
    )(x2d, g, w)


def _sb_project(x2d, g, w, tm):
    n = x2d.shape[0]
    row = lambda i: (i, 0)
    return pl.pallas_call(
        _sb_proj_kernel,
        grid=(n // tm,),
        in_specs=[pl.BlockSpec((tm, D_MODEL), row), _const_spec((1, D_MODEL)), _const_spec(w.shape)],
        out_specs=[pl.BlockSpec((tm, 3 * D_MODEL), row), pl.BlockSpec((tm, D_MODEL), row),
                   pl.BlockSpec((tm, D_MODEL), row)],
        out_shape=[jax.ShapeDtypeStruct((n, 3 * D_MODEL), BF16), jax.ShapeDtypeStruct((n, D_MODEL), F32),
                   jax.ShapeDtypeStruct((n, D_MODEL), F32)],
        compiler_params=_params(("parallel",)),
        name="sb_proj",
    )(x2d, g, w)


def _swa_proj_kernel(x_ref, g_ref, w_ref, b_ref, cos_ref, sin_ref, q_ref, k_ref, v_ref, kb_ref, vb_ref):
    h = _rms(x_ref[...], g_ref[...]).astype(BF16)
    y = _dot(h, w_ref[...]) + b_ref[...]
    cos = cos_ref[...]
    sin = sin_ref[...]
    nq = SWA_HEADS * SWA_HEAD_DIM
    nk = SWA_KV_HEADS * SWA_HEAD_DIM
    for c in range(nq // LANES):
        sl = slice(c * LANES, (c + 1) * LANES)
        q = y[:, :nq][:, sl] * cos + y[:, nq:2 * nq][:, sl] * sin
        q_ref[:, sl] = (q * SWA_SCALE).astype(BF16)
    o = 2 * nq
    for c in range(nk // LANES):
        sl = slice(c * LANES, (c + 1) * LANES)
        k = y[:, o:o + nk][:, sl] * cos + y[:, o + nk:o + 2 * nk][:, sl] * sin
        k_ref[:, sl] = k
        kb_ref[:, sl] = k.astype(BF16)
    v = y[:, o + 2 * nk:]
    v_ref[...] = v
    vb_ref[...] = v.astype(BF16)


def _swa_project(x2d, g, w, b, cos, sin, tm):
    n = x2d.shape[0]
    ntab = cos.shape[0] // tm
    nq = SWA_HEADS * SWA_HEAD_DIM
    nk = SWA_KV_HEADS * SWA_HEAD_DIM
    row = lambda i: (i, 0)
    tab = lambda i: (i % ntab, 0)
    return pl.pallas_call(
        _swa_proj_kernel,
        grid=(n // tm,),
        in_specs=[pl.BlockSpec((tm, D_MODEL), row), _const_spec((1, D_MODEL)), _const_spec(w.shape),
                  _const_spec(b.shape), pl.BlockSpec((tm, LANES), tab), pl.BlockSpec((tm, LANES), tab)],
        out_specs=[pl.BlockSpec((tm, nq), row), pl.BlockSpec((tm, nk), row), pl.BlockSpec((tm, nk), row),
                   pl.BlockSpec((tm, nk), row), pl.BlockSpec((tm, nk), row)],
        out_shape=[jax.ShapeDtypeStruct((n, nq), BF16), jax.ShapeDtypeStruct((n, nk), F32),
                   jax.ShapeDtypeStruct((n, nk), F32), jax.ShapeDtypeStruct((n, nk), BF16),
                   jax.ShapeDtypeStruct((n, nk), BF16)],
        compiler_params=_params(("parallel",)),
        name="swa_proj",
    )(x2d, g, w, b, cos, sin)


def _mla_attn_kernel(q_ref, k_ref, v_ref, o_ref, m_sc, l_sc, acc_sc, *, t):
    qi = pl.program_id(2)
    nh = q_ref.shape[0]
    m_sc[...] = jnp.full(m_sc.shape, NEG, F32)
    l_sc[...] = jnp.zeros(l_sc.shape, F32)
    acc_sc[...] = jnp.zeros(acc_sc.shape, F32)
    k_chunk = lax.broadcasted_iota(jnp.int32, (t, 1), 0) // CHUNK
    q_chunk = lax.broadcasted_iota(jnp.int32, (1, t), 1) // CHUNK
    diag_mask = k_chunk <= q_chunk

    def steps(blocks):
        work = [(j, mask, hd) for j, mask in blocks for hd in range(nh)]

        def scores(i):
            j, _, hd = work[i]
            return _dot(k_ref[0, pl.ds(pl.multiple_of(j * t, t), t), hd * MLA_KDIM:(hd + 1) * MLA_KDIM], q_ref[hd])

        ahead = 5
        pending = [scores(i) for i in range(ahead)]
        for i, (j, mask, hd) in enumerate(work):
            start = pl.multiple_of(j * t, t)
            s = pending.pop(0)
            if i + ahead < len(work):
                pending.append(scores(i + ahead))
            if mask is not None:
                s = jnp.where(mask, s, NEG)
            m_old = m_sc[hd]
            m_new = jnp.maximum(m_old, jnp.max(s, axis=0, keepdims=True))
            alpha = jnp.exp2(m_old - m_new)
            p = jnp.exp2(s - m_new)
            l_sc[hd] = alpha * l_sc[hd] + jnp.sum(p, axis=0, keepdims=True)
            v = v_ref[hd * MLA_V:(hd + 1) * MLA_V, pl.ds(start, t)]
            acc_sc[hd] = alpha * acc_sc[hd] + _dot(v, p.astype(BF16))
            m_sc[hd] = m_new

    def body(jj, _):
        steps([(2 * jj, None), (2 * jj + 1, None)])
        return 0

    lax.fori_loop(0, qi // 2, body, 0)

    @pl.when(qi % 2 == 1)
    def _():
        steps([(qi - 1, None), (qi, diag_mask)])

    @pl.when(qi % 2 == 0)
    def _():
        steps([(qi, diag_mask)])
    for hd in range(nh):
        o_ref[:, hd * MLA_V:(hd + 1) * MLA_V] = (acc_sc[hd] / l_sc[hd]).T.astype(BF16)


def _mla_attention(q_t, k_full, v_t, b, s, t):
    n = b * s
    nq = s // t
    g = MLA_HEAD_GROUP
    return pl.pallas_call(
        functools.partial(_mla_attn_kernel, t=t),
        grid=(b, MLA_HEADS // g, nq),
        in_specs=[
            pl.BlockSpec((g, MLA_KDIM, t), lambda bi, hg, qi: (hg, 0, bi * nq + qi)),
            pl.BlockSpec((1, s, g * MLA_KDIM), lambda bi, hg, qi: (bi, 0, hg)),
            pl.BlockSpec((g * MLA_V, s), lambda bi, hg, qi: (hg, bi)),
        ],
        out_specs=pl.BlockSpec((t, g * MLA_V), lambda bi, hg, qi: (bi * nq + qi, hg)),
        out_shape=jax.ShapeDtypeStruct((n, MLA_HEADS * MLA_V), BF16),
        scratch_shapes=[pltpu.VMEM((g, 1, t), F32), pltpu.VMEM((g, 1, t), F32),
                        pltpu.VMEM((g, MLA_V, t), F32)],
        compiler_params=_params(("parallel", "parallel", "arbitrary")),
        name="mla_attn",
    )(q_t, k_full.reshape(b, s, MLA_HEADS * MLA_KDIM), v_t)


def _mla_attn_sample_kernel(q_ref, cc_ref, cr_ref, kn_ref, wuv_ref, o_ref, *, t, past):
    rows = MLA_HEADS * t
    q = q_ref[...].reshape(rows, MLA_QK)
    ck = cc_ref[0].astype(BF16)
    s_old = (_dot_nt(q[:, :MLA_KV_LORA], ck)
             + _dot(q[:, MLA_KV_LORA:MLA_KV_LORA + MLA_ROPE], cr_ref[0].astype(BF16)))
    kn = kn_ref[...]
    s_new = _dot_nt(q, kn)
    tok = lax.broadcasted_iota(jnp.int32, (rows, 1), 0) & (t - 1)
    k_chunk = (past + lax.broadcasted_iota(jnp.int32, (1, t), 1)) // CHUNK
    s_new = jnp.where(k_chunk <= (past + tok) // CHUNK, s_new, NEG)
    m = jnp.maximum(jnp.max(s_old, axis=-1, keepdims=True), jnp.max(s_new, axis=-1, keepdims=True))
    p_old = jnp.exp(s_old - m)
    p_new = jnp.exp(s_new - m)
    l = jnp.sum(p_old, axis=-1, keepdims=True) + jnp.sum(p_new, axis=-1, keepdims=True)
    acc = _dot(p_old.astype(BF16), ck) + _dot(p_new.astype(BF16), kn[:, :MLA_KV_LORA])
    o_lat = (acc / l).astype(BF16)
    for hd in range(MLA_HEADS):
        o = _dot(o_lat[hd * t:(hd + 1) * t], wuv_ref[hd])
        o_ref[:, hd * MLA_V:(hd + 1) * MLA_V] = o.astype(BF16)


def _mla_attention_sample(q, cache_ckv, cache_kr, kcat, wuv, layer):
    _, db, past, _ = cache_ckv.shape
    t = q.shape[1] // db
    return pl.pallas_call(
        functools.partial(_mla_attn_sample_kernel, t=t, past=past),
        grid=(db,),
        in_specs=[
            pl.BlockSpec((MLA_HEADS, t, MLA_QK), lambda bi: (0, bi, 0)),
            pl.BlockSpec((None, 1, past, MLA_KV_LORA), lambda bi: (layer, bi, 0, 0)),
            pl.BlockSpec((None, 1, MLA_ROPE, past), lambda bi: (layer, bi, 0, 0)),
            pl.BlockSpec((t, MLA_QK), lambda bi: (bi, 0)),
            _const_spec(wuv.shape),
        ],
        out_specs=pl.BlockSpec((t, MLA_HEADS * MLA_V), lambda bi: (bi, 0)),
        out_shape=jax.ShapeDtypeStruct((db * t, MLA_HEADS * MLA_V), BF16),
        compiler_params=_params(("parallel",)),
        name="mla_attn_sample",
    )(q, cache_ckv, jnp.swapaxes(cache_kr, 2, 3), kcat, wuv)


def _later_key_matrix(tk):
    r = lax.broadcasted_iota(jnp.int32, (tk, tk), 0)
    c = lax.broadcasted_iota(jnp.int32, (tk, tk), 1)
    return jnp.where(r > c, 1.0, 0.0).astype(BF16)


def _split_pair(q2):
    lane = lax.broadcasted_iota(jnp.int32, (1, LANES), 1)
    zero = jnp.zeros_like(q2)
    return jnp.where(lane < LANES // 2, q2, zero), jnp.where(lane >= LANES // 2, q2, zero), lane


def _sb_logits_stage(z, mask):
    log_sig = jnp.minimum(z, 0.0) - jnp.log(1.0 + jnp.exp2(jnp.minimum(z, -z))) * LOG2E
    log_not = log_sig - z
    if mask is not None:
        log_not = jnp.where(mask, log_not, 0.0)
    hi = log_not.astype(BF16)
    lo = (log_not - hi.astype(F32)).astype(BF16)
    return log_sig, log_not, hi, lo


def _sb_attn_kernel(q_ref, k_ref, v_ref, o_ref, carry_sc, acc_sc, *, t):
    qi = pl.program_id(1)
    low = lax.broadcasted_iota(jnp.int32, (LANES, 1), 0) < LANES // 2
    r = lax.broadcasted_iota(jnp.int32, (t, t), 0)
    c = lax.broadcasted_iota(jnp.int32, (t, t), 1)
    later = jnp.where(c > r, 1.0, 0.0).astype(BF16)
    later2 = jnp.concatenate([later, later], axis=1)
    diag_mask = r < c
    carry_sc[...] = jnp.zeros(carry_sc.shape, F32)
    acc_sc[...] = jnp.zeros(acc_sc.shape, F32)

    def pick(x, hd):
        zero = jnp.zeros_like(x)
        return jnp.where(low, x, zero) if hd % 2 == 0 else jnp.where(low, zero, x)

    def blocks(items):
        work = [(j, mask, hd) for j, mask in items for hd in range(SB_HEADS)]

        def logits(i):
            j, _, hd = work[i]
            g = hd // 2
            qh = pick(q_ref[g * LANES:(g + 1) * LANES, :], hd)
            start = pl.multiple_of(j * t, t)
            return _dot(k_ref[0, pl.ds(start, t), g * LANES:(g + 1) * LANES], qh)

        def stage1(i, z):
            log_sig, log_not, hi, lo = _sb_logits_stage(z, work[i][1])
            tail = _dot(later2, jnp.concatenate([hi, lo], axis=0))
            return i, log_sig, tail, jnp.sum(log_not, axis=0, keepdims=True)

        def stage2(i, log_sig, tail, colsum):
            j, mask, hd = work[i]
            carry = carry_sc[hd]
            a = jnp.exp2(log_sig + (tail + carry))
            if mask is not None:
                a = jnp.where(mask, a, 0.0)
            carry_sc[hd] = carry + colsum
            g = hd // 2
            vh = pick(v_ref[g * LANES:(g + 1) * LANES, pl.ds(pl.multiple_of(j * t, t), t)], hd)
            acc_sc[g] += _dot(vh, a.astype(BF16))

        ahead = 2
        zs = [logits(i) for i in range(ahead)]
        pending = None
        for i in range(len(work)):
            st = stage1(i, zs.pop(0))
            if i + ahead < len(work):
                zs.append(logits(i + ahead))
            if pending is not None:
                stage2(*pending)
            pending = st
        stage2(*pending)

    @pl.when(qi % 2 == 1)
    def _():
        blocks([(qi, diag_mask), (qi - 1, None)])

    @pl.when(qi % 2 == 0)
    def _():
        blocks([(qi, diag_mask)])

    rest = qi - qi % 2

    def body(jj, _):
        blocks([(rest - 1 - 2 * jj, None), (rest - 2 - 2 * jj, None)])
        return 0

    lax.fori_loop(0, rest // 2, body, 0)
    for g in range(SB_HEADS // 2):
        o_ref[:, g * LANES:(g + 1) * LANES] = acc_sc[g].T.astype(BF16)


def _sb_attention(q_t, kb, v_t, b, s, t):
    n = b * s
    nq = s // t
    return pl.pallas_call(
        functools.partial(_sb_attn_kernel, t=t),
        grid=(b, nq),
        in_specs=[
            pl.BlockSpec((D_MODEL, t), lambda bi, qi: (0, bi * nq + qi)),
            pl.BlockSpec((1, s, D_MODEL), lambda bi, qi: (bi, 0, 0)),
            pl.BlockSpec((D_MODEL, s), lambda bi, qi: (0, bi)),
        ],
        out_specs=pl.BlockSpec((t, D_MODEL), lambda bi, qi: (bi * nq + qi, 0)),
        out_shape=jax.ShapeDtypeStruct((n, D_MODEL), BF16),
        scratch_shapes=[pltpu.VMEM((SB_HEADS, 1, t), F32), pltpu.VMEM((SB_HEADS // 2, LANES, t), F32)],
        compiler_params=_params(("parallel", "arbitrary")),
        name="sb_attn",
    )(q_t, kb.reshape(b, s, D_MODEL), v_t)


def _sb_attn_sample_kernel(q_ref, kn_ref, vn_ref, ck_ref, cv_ref, o_ref, carry_sc, acc_sc, *, t, tk):
    j = pl.program_id(1)
    hdim = SB_HEAD_DIM

    def head(ref, hd):
        return ref[:, hd * hdim:(hd + 1) * hdim]

    def run(logits, weighted, later2, mask, first):
        group = 4

        def stage1(heads, zs):
            parts = [_sb_logits_stage(z, mask) for z in zs]
            stacked = jnp.concatenate([jnp.concatenate([hi, lo], axis=1) for _, _, hi, lo in parts], axis=0)
            tails = _dot(stacked, later2)
            return [(hd, log_sig, tails[i * t:(i + 1) * t], jnp.sum(log_not, axis=-1, keepdims=True))
                    for i, (hd, (log_sig, log_not, _, _)) in enumerate(zip(heads, parts))]

        def stage2(hd, log_sig, tail, rowsum):
            if first:
                a = jnp.exp2(log_sig + tail)
                carry_sc[hd] = rowsum
            else:
                carry = carry_sc[hd]
                a = jnp.exp2(log_sig + (tail + carry))
                carry_sc[hd] = carry + rowsum
            if mask is not None:
                a = jnp.where(mask, a, 0.0)
            o = weighted(hd, a.astype(BF16))
            acc_sc[hd] = o if first else acc_sc[hd] + o

        groups = [list(range(g, g + group)) for g in range(0, SB_HEADS, group)]
        zs = [logits(hd) for hd in groups[0]]
        pending = []
        for gi, heads in enumerate(groups):
            nxt = [logits(hd) for hd in groups[gi + 1]] if gi + 1 < len(groups) else []
            done = stage1(heads, zs)
            zs = nxt
            for item in pending:
                stage2(*item)
            pending = done
        for item in pending:
            stage2(*item)

    def later_pair(n):
        u = _later_key_matrix(n)
        return jnp.concatenate([u, u], axis=0)

    @pl.when(j == 0)
    def _():
        r = lax.broadcasted_iota(jnp.int32, (t, t), 0)
        c = lax.broadcasted_iota(jnp.int32, (t, t), 1)
        run(lambda hd: _dot_nt(head(q_ref, hd), head(kn_ref, hd)),
            lambda hd, a: _dot(a, head(vn_ref, hd)), later_pair(t), c < r, True)

    run(lambda hd: _dot(head(q_ref, hd), ck_ref[0, hd].astype(BF16)),
        lambda hd, a: _dot_nt(a, cv_ref[0, hd].astype(BF16)), later_pair(tk), None, False)

    @pl.when(j == pl.num_programs(1) - 1)
    def _():
        o_ref[...] = jnp.concatenate([acc_sc[hd] for hd in range(SB_HEADS)], axis=-1).astype(BF16)


def _sb_attention_sample(qkv, cache_k, cache_v, layer, db, tk):
    n = qkv.shape[0]
    t = n // db
    past = cache_k.shape[2]
    nblk = past // tk
    cache = lambda bi, j: (layer, bi, 0, 0, nblk - 1 - j)
    cache_k = jnp.transpose(cache_k, (0, 1, 3, 4, 2))
    cache_v = jnp.transpose(cache_v, (0, 1, 3, 4, 2))
    return pl.pallas_call(
        functools.partial(_sb_attn_sample_kernel, t=t, tk=tk),
        grid=(db, nblk),
        in_specs=[
            pl.BlockSpec((t, D_MODEL), lambda bi, j: (bi, 0)),
            pl.BlockSpec((t, D_MODEL), lambda bi, j: (bi, 1)),
            pl.BlockSpec((t, D_MODEL), lambda bi, j: (bi, 2)),
            pl.BlockSpec((None, 1, SB_HEADS, SB_HEAD_DIM, tk), cache),
            pl.BlockSpec((None, 1, SB_HEADS, SB_HEAD_DIM, tk), cache),
        ],
        out_specs=pl.BlockSpec((t, D_MODEL), lambda bi, j: (bi, 0)),
        out_shape=jax.ShapeDtypeStruct((n, D_MODEL), BF16),
        scratch_shapes=[pltpu.VMEM((SB_HEADS, t, 1), F32), pltpu.VMEM((SB_HEADS, t, SB_HEAD_DIM), F32)],
        compiler_params=_params(("parallel", "arbitrary")),
        name="sb_attn_sample",
    )(qkv, qkv, qkv, cache_k, cache_v)


def _swa_heads(sink_ref, q_ref, kband, vband, mask, o_ref):
    lane = lax.broadcasted_iota(jnp.int32, (1, LANES), 1)

    def scores(hd):
        blk, half = divmod(hd, 2)
        pair = blk // SWA_GROUP
        qh = _split_pair(q_ref[:, blk * LANES:(blk + 1) * LANES])[half]
        return _dot_nt(qh, kband[:, pair * LANES:(pair + 1) * LANES])

    ahead = 3
    pending = [scores(hd) for hd in range(ahead)]
    outs = []
    for hd in range(SWA_HEADS):
        blk, half = divmod(hd, 2)
        pair = blk // SWA_GROUP
        s = jnp.where(mask, pending.pop(0), NEG)
        if hd + ahead < SWA_HEADS:
            pending.append(scores(hd + ahead))
        sink = sink_ref[hd]
        m = jnp.maximum(jnp.max(s, axis=-1, keepdims=True), sink)
        e = jnp.exp(s - m)
        den = jnp.sum(e, axis=-1, keepdims=True) + jnp.exp(sink - m)
        outs.append(_dot((e / den).astype(BF16), vband[:, pair * LANES:(pair + 1) * LANES]))
        if half:
            o_ref[:, blk * LANES:(blk + 1) * LANES] = jnp.where(lane < LANES // 2, outs[-2], outs[-1]).astype(BF16)


def _swa_attn_kernel(sink_ref, q_ref, kp_ref, kc_ref, vp_ref, vc_ref, o_ref, *, tq):
    i = pl.program_id(1)
    kband = jnp.concatenate([kp_ref[0], kc_ref[0]], axis=0)
    vband = jnp.concatenate([vp_ref[0], vc_ref[0]], axis=0)
    q_pos = i * tq + lax.broadcasted_iota(jnp.int32, (tq, 1), 0)
    k_pos = i * tq - SWA_WINDOW + lax.broadcasted_iota(jnp.int32, (1, SWA_WINDOW + tq), 1)
    dc = (q_pos + SWA_WINDOW) // CHUNK - (k_pos + SWA_WINDOW) // CHUNK
    mask = (dc >= 0) & (dc <= SWA_WINDOW // CHUNK) & (k_pos >= 0)
    _swa_heads(sink_ref, q_ref, kband, vband, mask, o_ref)


def _swa_attention(sinks, q, kb, vb, b, s):
    tq = SWA_WINDOW
    nq = s // tq
    nk = SWA_KV_HEADS * SWA_HEAD_DIM
    kb3 = kb.reshape(b, s, nk)
    vb3 = vb.reshape(b, s, nk)
    prev = lambda bi, qi: (bi, jnp.maximum(qi - 1, 0), 0)
    cur = lambda bi, qi: (bi, qi, 0)
    return pl.pallas_call(
        functools.partial(_swa_attn_kernel, tq=tq),
        grid=(b, nq),
        in_specs=[
            pl.BlockSpec(memory_space=pltpu.SMEM),
            pl.BlockSpec((tq, D_MODEL), lambda bi, qi: (bi * nq + qi, 0)),
            pl.BlockSpec((1, tq, nk), prev), pl.BlockSpec((1, tq, nk), cur),
            pl.BlockSpec((1, tq, nk), prev), pl.BlockSpec((1, tq, nk), cur),
        ],
        out_specs=pl.BlockSpec((tq, D_MODEL), lambda bi, qi: (bi * nq + qi, 0)),
        out_shape=jax.ShapeDtypeStruct((b * s, D_MODEL), BF16),
        compiler_params=_params(("parallel", "arbitrary")),
        name="swa_attn",
    )(sinks, q, kb3, kb3, vb3, vb3)


def _swa_attn_sample_kernel(sink_ref, q_ref, ck_ref, cv_ref, kn_ref, vn_ref, o_ref, *, t, past):
    clen = ck_ref.shape[1]
    kband = jnp.concatenate([ck_ref[0].astype(BF16), kn_ref[...]], axis=0)
    vband = jnp.concatenate([cv_ref[0].astype(BF16), vn_ref[...]], axis=0)
    q_pos = past + lax.broadcasted_iota(jnp.int32, (t, 1), 0)
    k_pos = past - clen + lax.broadcasted_iota(jnp.int32, (1, clen + t), 1)
    dc = q_pos // CHUNK - (k_pos + SWA_WINDOW) // CHUNK + SWA_WINDOW // CHUNK
    mask = (dc >= 0) & (dc <= SWA_WINDOW // CHUNK) & (k_pos >= 0)
    _swa_heads(sink_ref, q_ref, kband, vband, mask, o_ref)


def _swa_attention_sample(sinks, q, cache_k, cache_v, kb, vb, layer, db, past):
    n = q.shape[0]
    t = n // db
    clen = cache_k.shape[2]
    nk = SWA_KV_HEADS * SWA_HEAD_DIM
    ck = cache_k.reshape(cache_k.shape[0], db, clen, nk)
    cv = cache_v.reshape(cache_v.shape[0], db, clen, nk)
    row = lambda bi: (bi, 0)
    cache = lambda bi: (layer, bi, 0, 0)
    return pl.pallas_call(
        functools.partial(_swa_attn_sample_kernel, t=t, past=past),
        grid=(db,),
        in_specs=[
            pl.BlockSpec(memory_space=pltpu.SMEM),
            pl.BlockSpec((t, D_MODEL), row),
            pl.BlockSpec((None, 1, clen, nk), cache), pl.BlockSpec((None, 1, clen, nk), cache),
            pl.BlockSpec((t, nk), row), pl.BlockSpec((t, nk), row),
        ],
        out_specs=pl.BlockSpec((t, D_MODEL), row),
        out_shape=jax.ShapeDtypeStruct((n, D_MODEL), BF16),
        compiler_params=_params(("parallel",)),
        name="swa_attn_sample",
    )(sinks, q, ck, cv, kb, vb)


def _post_kernel(x_ref, o_ref, wo_ref, gf_ref, wg_ref, wu_ref, cw_ref, cb_ref, wout_ref, st_ref,
                 p_ref, gp_ref, wpg_ref, wpp_ref, gfin_ref, y_ref, cs_ref, carry_sc, *, final):
    ti = pl.program_id(1)
    ns, tt, _ = x_ref.shape
    rows = ns * tt
    x = x_ref[...].reshape(rows, D_MODEL)
    x = x + _dot(o_ref[...].reshape(rows, o_ref.shape[-1]), wo_ref[...])
    h = _rms(x, gf_ref[...]).astype(BF16)
    t_idx = lax.broadcasted_iota(jnp.int32, (1, tt, 1), 1)

    @pl.when(ti == 0)
    def _():
        carry_sc[...] = st_ref[...]

    proj = _dot(p_ref[...].reshape(rows, PLE_DIM).astype(BF16), wpp_ref[...])
    acc = jnp.zeros((rows, D_MODEL), F32)
    ahead = 2
    nxt = [(_dot(h, wg_ref[c]), _dot(h, wu_ref[c])) for c in range(ahead)]
    for c in range(N_FF_CHUNKS):
        sl = slice(c * FF_CHUNK, (c + 1) * FF_CHUNK)
        gate, up = nxt.pop(0)
        if c + ahead < N_FF_CHUNKS:
            nxt.append((_dot(h, wg_ref[c + ahead]), _dot(h, wu_ref[c + ahead])))
        st = carry_sc[:, :, sl]
        g1 = pltpu.roll(gate, 1, 0).reshape(ns, tt, FF_CHUNK)
        g2 = pltpu.roll(gate, 2, 0).reshape(ns, tt, FF_CHUNK)
        gate = gate.reshape(ns, tt, FF_CHUNK)
        g1 = jnp.where(t_idx == 0, st[:, 1:2, :], g1)
        g2 = jnp.where(t_idx == 0, st[:, 0:1, :], jnp.where(t_idx == 1, st[:, 1:2, :], g2))
        cw = cw_ref[:, sl]
        conv = g2 * cw[0:1, :] + g1 * cw[1:2, :] + gate * cw[2:3, :] + cb_ref[:, sl]
        new_st = gate[:, tt - 2:, :]
        carry_sc[:, :, sl] = new_st
        cs_ref[:, :, sl] = new_st
        act = 0.5 * conv * (1.0 + lax.erf(conv * (0.5 ** 0.5)))
        y = (act * up.reshape(ns, tt, FF_CHUNK)).reshape(rows, FF_CHUNK)
        acc = acc + _dot(y.astype(BF16), wout_ref[c])
    x = x + acc
    hp = _rms(x, gp_ref[...]).astype(BF16)
    gate_p = jax.nn.sigmoid(_dot(hp, wpg_ref[...]))
    x = x + gate_p * proj
    if final:
        x = _rms(x, gfin_ref[...])
    y_ref[...] = x.reshape(ns, tt, D_MODEL)


def _post(x3, o2d, w, state, p4, layer, g_final, ns, tt, final):
    bx, tx, _ = x3.shape
    do = o2d.shape[-1]
    o3 = o2d.reshape(bx, tx, do)
    blk = lambda bi, ti: (bi, ti, 0)
    return pl.pallas_call(
        functools.partial(_post_kernel, final=final),
        grid=(bx // ns, tx // tt),
        in_specs=[
            pl.BlockSpec((ns, tt, D_MODEL), blk),
            pl.BlockSpec((ns, tt, do), blk),
            _const_spec(w["wo"].shape),
            _const_spec((1, D_MODEL)),
            _const_spec(w["wg"].shape),
            _const_spec(w["wu"].shape),
            _const_spec(w["cw"].shape),
            _const_spec(w["cb"].shape),
            _const_spec(w["wout"].shape),
            pl.BlockSpec((ns, 2, D_FF), lambda bi, ti: (bi, 0, 0)),
            pl.BlockSpec((None, ns, tt, PLE_DIM), lambda bi, ti: (layer, bi, ti, 0)),
            _const_spec((1, D_MODEL)),
            _const_spec(w["wpg"].shape),
            _const_spec(w["wpp"].shape),
            _const_spec((1, D_MODEL)),
        ],
        out_specs=[pl.BlockSpec((ns, tt, D_MODEL), blk),
                   pl.BlockSpec((ns, 2, D_FF), lambda bi, ti: (bi, 0, 0))],
        out_shape=[jax.ShapeDtypeStruct(x3.shape, F32), jax.ShapeDtypeStruct((bx, 2, D_FF), F32)],
        scratch_shapes=[pltpu.VMEM((ns, 2, D_FF), F32)],
        compiler_params=_params(("parallel", "arbitrary")),
        name="post_ffn",
    )(x3, o3, w["wo"], w["gf"], w["wg"], w["wu"], w["cw"], w["cb"], w["wout"], state, p4,
      w["gp"], w["wpg"], w["wpp"], g_final)


def _rot_cols(w):
    half = w.shape[-1] // 2
    return jnp.concatenate([-w[..., half:], w[..., :half]], axis=-1)


def _rope_tables(pos):
    half = MLA_ROPE // 2
    inv = ROPE_THETA ** (-jnp.arange(half, dtype=F32) / half)
    ang = pos.astype(F32)[:, None] * inv[None, :]
    reps = LANES // half
    return jnp.tile(jnp.cos(ang), (1, reps)), jnp.tile(jnp.sin(ang), (1, reps))


def _prep_mla(w_dq, g_q, w_uq, w_dkv, g_kv, w_uk, w_uv):
    z = jnp.zeros((D_MODEL, LANES - MLA_ROPE), F32)
    kr = w_dkv[:, MLA_KV_LORA:]
    wh = jnp.concatenate([w_dq, w_dkv[:, :MLA_KV_LORA], kr, z, _rot_cols(kr), z], axis=1)
    uq = w_uq.reshape(MLA_Q_LORA, MLA_HEADS, MLA_NOPE + MLA_ROPE)
    rp = uq[:, :, MLA_NOPE:]
    pad = jnp.zeros((MLA_Q_LORA, MLA_HEADS, LANES - MLA_ROPE), F32)
    nh = MLA_HEADS * LANES
    wuq = jnp.concatenate([uq[:, :, :MLA_NOPE].reshape(MLA_Q_LORA, nh),
                           jnp.concatenate([rp, pad], -1).reshape(MLA_Q_LORA, nh),
                           jnp.concatenate([_rot_cols(rp), pad], -1).reshape(MLA_Q_LORA, nh)], axis=1)
    wuq = wuq.astype(BF16)
    return dict(wh=wh.astype(BF16), gq=g_q[None, :], wuq=wuq, wuq_t=wuq.T, gkv=g_kv[None, :],
                wuk=jnp.transpose(w_uk, (1, 2, 0)).astype(BF16),
                wuv=jnp.transpose(w_uv, (1, 0, 2)).astype(BF16),
                wuk_flat=w_uk.reshape(MLA_KV_LORA, MLA_HEADS * MLA_NOPE).astype(BF16),
                wuv_flat_t=w_uv.reshape(MLA_KV_LORA, MLA_HEADS * MLA_V).T.astype(BF16))


def _prep_swa(w_qkv, b_qkv, sinks, w_o):
    nq = SWA_HEADS * SWA_HEAD_DIM
    nk = SWA_KV_HEADS * SWA_HEAD_DIM
    perm = jnp.array(SWA_PERM)

    def split(a):
        lead = a.shape[:-1]
        q = a[..., :nq].reshape(*lead, SWA_HEADS, SWA_HEAD_DIM)[..., perm, :]
        k = a[..., nq:nq + nk].reshape(*lead, SWA_KV_HEADS, SWA_HEAD_DIM)
        return jnp.concatenate([q.reshape(*lead, nq), _rot_cols(q).reshape(*lead, nq),
                                k.reshape(*lead, nk), _rot_cols(k).reshape(*lead, nk), a[..., nq + nk:]], axis=-1)

    wo = w_o.reshape(SWA_HEADS, SWA_HEAD_DIM, D_MODEL)[perm].reshape(nq, D_MODEL)
    return split(w_qkv).astype(BF16), split(b_qkv)[None, :], sinks[perm], wo.astype(BF16)


def _prep_post(w_o, g_ffn, w_in, conv_w, conv_b, w_out, g_ple, w_gate, w_proj):
    def chunks(w):
        return jnp.transpose(w.reshape(D_MODEL, N_FF_CHUNKS, FF_CHUNK), (1, 0, 2)).astype(BF16)
    return dict(wo=w_o.astype(BF16), gf=g_ffn[None, :], wg=chunks(w_in[:, :D_FF]), wu=chunks(w_in[:, D_FF:]),
                cw=conv_w, cb=conv_b[None, :], wout=w_out.reshape(N_FF_CHUNKS, FF_CHUNK, D_MODEL).astype(BF16),
                gp=g_ple[None, :], wpg=w_gate.astype(BF16), wpp=w_proj.astype(BF16))


def kernel(x_prompt, x_sample, p_prompt, p_sample, cache_mla_ckv, cache_mla_krope, cache_sb_k, cache_sb_v, cache_swa_k, cache_swa_v, state_ffn_conv, g_mix, g_ffn, g_ple, g_final, w_mla_dq, g_mla_q, w_mla_uq, w_mla_dkv, g_mla_kv, w_mla_uk, w_mla_uv, w_mla_o, w_sb_qkv, w_sb_o, w_swa_qkv, b_swa_qkv, swa_sinks, w_swa_o, w_ffn_in, ffn_conv_w, ffn_conv_b, w_ffn_out, w_ple_gate, w_ple_proj):
    b, s, _ = x_prompt.shape
    db, t, _ = x_sample.shape
    depth = g_mix.shape[0]
    past = cache_mla_ckv.shape[2]
    n_p = b * s
    n_s = db * t
    assert s % 512 == 0 and t == CHUNK and past % 256 == 0 and n_s % 512 == 0
    assert cache_swa_k.shape[2] == SWA_WINDOW and 512 % t == 0

    tm_mla = 256
    tm = 512
    tt = 512
    ns = 512 // t
    cos_p, sin_p = _rope_tables(jnp.arange(s, dtype=jnp.int32))
    cos_s, sin_s = _rope_tables(past + (jnp.arange(tm, dtype=jnp.int32) % t))
    zero_state = jnp.zeros((b, 2, D_FF), F32)
    g_fin = g_final[None, :]

    xp, xs = x_prompt, x_sample
    outs = {k: [] for k in ("ckv_p", "kr_p", "ckv_s", "kr_s", "sbk_p", "sbv_p", "sbk_s", "sbv_s",
                            "swk_p", "swv_p", "swk_s", "swv_s", "conv_p", "conv_s")}
    for i in range(depth):
        j = i // N_MIXERS
        gm = g_mix[i][None, :]
        xp2 = xp.reshape(n_p, D_MODEL)
        xs2 = xs.reshape(n_s, D_MODEL)
        if i % N_MIXERS == 0:
            w = _prep_mla(w_mla_dq[j], g_mla_q[j], w_mla_uq[j], w_mla_dkv[j], g_mla_kv[j], w_mla_uk[j], w_mla_uv[j])
            q_p, ckv_p, kr_p, k_p, v_p = _mla_project_t(xp2, gm, w, cos_p, sin_p, tm_mla)
            q_s, ckv_s, kr_s, kcat_s = _mla_project(xs2, gm, w, cos_s[:tm_mla], sin_s[:tm_mla], tm_mla)
            o_p = _mla_attention(q_p, k_p, v_p, b, s, 256)
            o_s = _mla_attention_sample(q_s, cache_mla_ckv, cache_mla_krope, kcat_s, w["wuv"], j)
            outs["ckv_p"].append(ckv_p.reshape(b, s, MLA_KV_LORA))
            outs["kr_p"].append(kr_p.reshape(b, s, MLA_ROPE))
            outs["ckv_s"].append(ckv_s.reshape(db, t, MLA_KV_LORA))
            outs["kr_s"].append(kr_s.reshape(db, t, MLA_ROPE))
            w_o = w_mla_o[j]
        elif i % N_MIXERS == 1:
            wq = w_sb_qkv[j].astype(BF16)
            q_t, kb_p, v_t, k_p, v_p = _sb_project_t(xp2, gm, wq, b, tm)
            qkv_s, k_s, v_s = _sb_project(xs2, gm, wq, tm)
            o_p = _sb_attention(q_t, kb_p, v_t, b, s, 256)
            o_s = _sb_attention_sample(qkv_s, cache_sb_k, cache_sb_v, j, db, 256)
            to_cache = lambda a: jnp.transpose(a.reshape(b, SB_HEADS, SB_HEAD_DIM, s), (0, 3, 1, 2))
            outs["sbk_p"].append(to_cache(k_p))
            outs["sbv_p"].append(to_cache(v_p))
            outs["sbk_s"].append(k_s.reshape(db, t, SB_HEADS, SB_HEAD_DIM))
            outs["sbv_s"].append(v_s.reshape(db, t, SB_HEADS, SB_HEAD_DIM))
            w_o = w_sb_o[j]
        else:
            wq, bq, sinks, w_o = _prep_swa(w_swa_qkv[j], b_swa_qkv[j], swa_sinks[j], w_swa_o[j])
            q_p, k_p, v_p, kb_p, vb_p = _swa_project(xp2, gm, wq, bq, cos_p, sin_p, tm)
            q_s, k_s, v_s, kb_s, vb_s = _swa_project(xs2, gm, wq, bq, cos_s, sin_s, tm)
            o_p = _swa_attention(sinks, q_p, kb_p, vb_p, b, s)
            o_s = _swa_attention_sample(sinks, q_s, cache_swa_k, cache_swa_v, kb_s, vb_s, j, db, past)
            kv_shape = (SWA_KV_HEADS, SWA_HEAD_DIM)
            outs["swk_p"].append(k_p.reshape(b, s, *kv_shape)[:, -SWA_WINDOW:])
            outs["swv_p"].append(v_p.reshape(b, s, *kv_shape)[:, -SWA_WINDOW:])
            outs["swk_s"].append(jnp.concatenate([cache_swa_k[j], k_s.reshape(db, t, *kv_shape)], 1)[:, -SWA_WINDOW:])
            outs["swv_s"].append(jnp.concatenate([cache_swa_v[j], v_s.reshape(db, t, *kv_shape)], 1)[:, -SWA_WINDOW:])
        wp = _prep_post(w_o, g_ffn[i], w_ffn_in[i], ffn_conv_w[i], ffn_conv_b[i], w_ffn_out[i],
                        g_ple[i], w_ple_gate[i], w_ple_proj[i])
        final = i == depth - 1
        xp, cp = _post(xp, o_p, wp, zero_state, p_prompt, i, g_fin, 1, tt, final)
        xs, cs = _post(xs, o_s, wp, state_ffn_conv[i], p_sample, i, g_fin, ns, t, final)
        outs["conv_p"].append(cp)
        outs["conv_s"].append(cs)
    order = ("ckv_p", "kr_p", "ckv_s", "kr_s", "sbk_p", "sbv_p", "sbk_s", "sbv_s",
             "swk_p", "swv_p", "swk_s", "swv_s", "conv_p", "conv_s")
    return (xp, xs) + tuple(jnp.stack(outs[k]) for k in order)
```
